```python
import math
import jax, jax.numpy as jnp
from jax import lax
import numpy as np

D_MODEL = 1024
BATCH = 8
SEQ = 2048
DEPTH = 4

GRID_W = 64
CTX_LEN = 256
Q_BLOCK = 128
ROPE_THETA = 10000.0
EPS = 1e-6

GQA_HEADS = 6
GQA_KV_HEADS = 2
GQA_HEAD_DIM = 64
GQA_WIDTH = GQA_HEADS * GQA_HEAD_DIM
DIFF_HEADS = 4
DIFF_QK_DIM = 32
DIFF_V_DIM = 64
DIFF_WIDTH = DIFF_HEADS * DIFF_V_DIM
MLA_HEADS = 6
MLA_NOPE = 64
MLA_ROPE = 32
MLA_V = 64
MLA_Q_RANK = 192
MLA_KV_RANK = 128
MLA_WIDTH = MLA_HEADS * MLA_V
MIX_WIDTH = GQA_WIDTH + DIFF_WIDTH + MLA_WIDTH

IN_SIZES = (
    GQA_HEADS * GQA_HEAD_DIM,
    GQA_KV_HEADS * GQA_HEAD_DIM,
    GQA_KV_HEADS * GQA_HEAD_DIM,
    2 * DIFF_HEADS * DIFF_QK_DIM,
    2 * DIFF_HEADS * DIFF_QK_DIM,
    DIFF_HEADS * DIFF_V_DIM,
    MLA_Q_RANK,
    MLA_KV_RANK,
    MLA_ROPE,
    GQA_WIDTH,
    DIFF_WIDTH,
    MLA_WIDTH,
)
IN_COLS = int(sum(IN_SIZES))
IN_SPLITS = tuple(int(s) for s in np.cumsum(IN_SIZES)[:-1])

DEEPNORM_ALPHA = (2.0 * DEPTH) ** 0.25
DEEPNORM_BETA = (8.0 * DEPTH) ** -0.25
GQA_SCALE = GQA_HEAD_DIM ** -0.5
DIFF_SCALE = DIFF_QK_DIM ** -0.5
MLA_SCALE = (MLA_NOPE + MLA_ROPE) ** -0.5

kernel_name = "hybrid_parallel_heads_dit_block"


def rms_norm(x, g):
    xf = x.astype(jnp.float32)
    y = xf * lax.rsqrt(jnp.mean(xf * xf, axis=-1, keepdims=True) + EPS)
    return (y * g.astype(jnp.float32)).astype(x.dtype)


def layer_norm(x, g=None, b=None):
    xf = x.astype(jnp.float32)
    mu = jnp.mean(xf, axis=-1, keepdims=True)
    var = jnp.mean(jnp.square(xf - mu), axis=-1, keepdims=True)
    y = (xf - mu) * lax.rsqrt(var + EPS)
    if g is not None:
        y = y * g.astype(jnp.float32) + b.astype(jnp.float32)
    return y.astype(x.dtype)


def rope_tables(pos, dim, dtype):
    freqs = ROPE_THETA ** (-jnp.arange(0, dim, 2, dtype=jnp.float32) / dim)
    ang = pos.astype(jnp.float32)[:, None] * freqs[None, :]
    return (jnp.cos(ang).astype(dtype), jnp.sin(ang).astype(dtype))


def axial_tables(row, col, head_rot_dim, dtype):
    half = head_rot_dim // 2
    return rope_tables(row, half, dtype) + rope_tables(col, half, dtype)


def rope_rotate(x, cos, sin):
    a, b = jnp.split(x, 2, axis=-1)
    cs = cos[None, :, None, :]
    sn = sin[None, :, None, :]
    return jnp.concatenate([a * cs - b * sn, a * sn + b * cs], axis=-1)


def axial_rope(x, tabs):
    cos_r, sin_r, cos_c, sin_c = tabs
    xr, xc = jnp.split(x, 2, axis=-1)
    return jnp.concatenate([rope_rotate(xr, cos_r, sin_r), rope_rotate(xc, cos_c, sin_c)], axis=-1)


def block_attention(q, k, v, scale):
    B, S, H, dq = q.shape
    Hk = k.shape[2]
    G = H // Hk
    nb = S // Q_BLOCK
    qb = jnp.moveaxis(q.reshape(B, nb, Q_BLOCK, Hk, G, dq), 1, 0)

    def one_block(qblk):
        s = jnp.einsum('bqkgd,blkd->bkgql', qblk, k).astype(jnp.float32) * scale
        p = jax.nn.softmax(s, axis=-1).astype(v.dtype)
        return jnp.einsum('bkgql,blkd->bqkgd', p, v)

    o = lax.map(one_block, qb)
    return jnp.moveaxis(o, 0, 1).reshape(B, S, H, v.shape[-1])


def gqa_q(z, g, tabs):
    B, L, _ = z.shape
    q = rms_norm(z.reshape(B, L, GQA_HEADS, GQA_HEAD_DIM), g)
    return q if tabs is None else axial_rope(q, tabs)


def gqa_kv(zk, zv, g, tabs):
    B, L, _ = zk.shape
    k = rms_norm(zk.reshape(B, L, GQA_KV_HEADS, GQA_HEAD_DIM), g)
    if tabs is not None:
        k = axial_rope(k, tabs)
    return k, zv.reshape(B, L, GQA_KV_HEADS, GQA_HEAD_DIM)


def diff_qk(z, tabs):
    B, L, _ = z.shape
    t = z.reshape(B, L, 2 * DIFF_HEADS, DIFF_QK_DIM)
    if tabs is not None:
        t = axial_rope(t, tabs)
    return t[:, :, 0::2], t[:, :, 1::2]


def diff_v(z):
    B, L, _ = z.shape
    return z.reshape(B, L, DIFF_HEADS, DIFF_V_DIM)


def diff_out(o1, o2, lam, lam_init, subln_g):
    o = o1 - lam.astype(o1.dtype) * o2
    return rms_norm(o, subln_g) * (1.0 - lam_init)


def mla_q(zcq, g, w_uq, tabs):
    B, L, _ = zcq.shape
    q = (rms_norm(zcq, g) @ w_uq).reshape(B, L, MLA_HEADS, MLA_NOPE + MLA_ROPE)
    q_nope, q_pe = jnp.split(q, [MLA_NOPE], axis=-1)
    if tabs is not None:
        q_pe = axial_rope(q_pe, tabs)
    return jnp.concatenate([q_nope, q_pe], axis=-1)


def mla_kv(zckv, zkr, g, w_ukv, tabs):
    B, L, _ = zckv.shape
    kv = (rms_norm(zckv, g) @ w_ukv).reshape(B, L, MLA_HEADS, MLA_NOPE + MLA_V)
    k_nope, v = jnp.split(kv, [MLA_NOPE], axis=-1)
    k_pe = zkr.reshape(B, L, 1, MLA_ROPE)
    if tabs is not None:
        k_pe = axial_rope(k_pe, tabs)
    k = jnp.concatenate([k_nope, jnp.broadcast_to(k_pe, (B, L, MLA_HEADS, MLA_ROPE))], axis=-1)
    return k, v


def merge_groups(o_a, o_b, o_c, g_a, g_b, g_c, w_out):
    B, L = o_a.shape[:2]
    h = jnp.concatenate([
        o_a.reshape(B, L, GQA_WIDTH) * jax.nn.silu(g_a),
        o_b.reshape(B, L, DIFF_WIDTH) * jax.nn.silu(g_b),
        o_c.reshape(B, L, MLA_WIDTH) * jax.nn.silu(g_c),
    ], axis=-1)
    return h @ w_out


def hybrid_layer(xl, xc, c, c_ctx, p, lam_init, tabs64, tabs32, ctx_out):
    sh_l, sc_l, gt_l = jnp.split(jax.nn.silu(c) @ p["w_ada"] + p["b_ada"], 3, axis=-1)
    sh_c, sc_c, gt_c = jnp.split(jax.nn.silu(c_ctx) @ p["w_ada"] + p["b_ada"], 3, axis=-1)
    hl = layer_norm(xl) * (1 + sc_l[:, None, :]) + sh_l[:, None, :]
    hc = layer_norm(xc) * (1 + sc_c) + sh_c

    (gq_l, gk_l, gv_l, dq_l, dk_l, dv_l, mq_l, mkv_l, mkr_l, ga_l, gb_l, gc_l) = jnp.split(hl @ p["w_in"], IN_SPLITS, axis=-1)
    (gq_c, gk_c, gv_c, dq_c, dk_c, dv_c, mq_c, mkv_c, mkr_c, ga_c, gb_c, gc_c) = jnp.split(hc @ p["w_in"], IN_SPLITS, axis=-1)

    lam = (jnp.exp(jnp.sum(p["diff_lq1"].astype(jnp.float32) * p["diff_lk1"].astype(jnp.float32)))
           - jnp.exp(jnp.sum(p["diff_lq2"].astype(jnp.float32) * p["diff_lk2"].astype(jnp.float32)))
           + lam_init)

    kc_a, vc_a = gqa_kv(gk_c, gv_c, p["gqa_k_g"], None)
    k1c, k2c = diff_qk(dk_c, None)
    vc_b = diff_v(dv_c)
    kc_m, vc_m = mla_kv(mkv_c, mkr_c, p["mla_kv_g"], p["w_ukv"], None)

    ql_a = gqa_q(gq_l, p["gqa_q_g"], tabs64)
    kl_a, vl_a = gqa_kv(gk_l, gv_l, p["gqa_k_g"], tabs64)
    o_a = block_attention(ql_a, jnp.concatenate([kc_a, kl_a], 1), jnp.concatenate([vc_a, vl_a], 1), GQA_SCALE)

    q1l, q2l = diff_qk(dq_l, tabs32)
    k1l, k2l = diff_qk(dk_l, tabs32)
    v_all_b = jnp.concatenate([vc_b, diff_v(dv_l)], 1)
    o_b = diff_out(block_attention(q1l, jnp.concatenate([k1c, k1l], 1), v_all_b, DIFF_SCALE),
                   block_attention(q2l, jnp.concatenate([k2c, k2l], 1), v_all_b, DIFF_SCALE),
                   lam, lam_init, p["diff_subln_g"])

    ql_m = mla_q(mq_l, p["mla_q_g"], p["w_uq"], tabs32)
    kl_m, vl_m = mla_kv(mkv_l, mkr_l, p["mla_kv_g"], p["w_ukv"], tabs32)
    o_m = block_attention(ql_m, jnp.concatenate([kc_m, kl_m], 1), jnp.concatenate([vc_m, vl_m], 1), MLA_SCALE)

    y_l = merge_groups(o_a, o_b, o_m, ga_l, gb_l, gc_l, p["w_out"])
    xl_new = layer_norm(DEEPNORM_ALPHA * xl + gt_l[:, None, :] * y_l, p["ln_g"], p["ln_b"])
    if not ctx_out:
        return xl_new, None

    oc_a = block_attention(gqa_q(gq_c, p["gqa_q_g"], None), kc_a, vc_a, GQA_SCALE)
    q1c, q2c = diff_qk(dq_c, None)
    oc_b = diff_out(block_attention(q1c, k1c, vc_b, DIFF_SCALE),
                    block_attention(q2c, k2c, vc_b, DIFF_SCALE),
                    lam, lam_init, p["diff_subln_g"])
    oc_m = block_attention(mla_q(mq_c, p["mla_q_g"], p["w_uq"], None), kc_m, vc_m, MLA_SCALE)
    y_c = merge_groups(oc_a, oc_b, oc_m, ga_c, gb_c, gc_c, p["w_out"])
    xc_new = layer_norm(DEEPNORM_ALPHA * xc + gt_c * y_c, p["ln_g"], p["ln_b"])
    return xl_new, xc_new


def setup_inputs(seed: int = 0) -> dict:
    key = jax.random.key(seed)
    ks = jax.random.split(key, 21)

    def nrm(k, shape, s):
        return jax.random.normal(k, shape, jnp.float32) * s

    def gain(k, shape):
        return 1.0 + 0.02 * jax.random.normal(k, shape, jnp.float32)

    return {
        "x": nrm(ks[0], (BATCH, SEQ, D_MODEL), 1.0),
        "c": nrm(ks[1], (BATCH, D_MODEL), 1.0),
        "ctx": nrm(ks[2], (BATCH, CTX_LEN, D_MODEL), 1.0),
        "c_ctx": nrm(ks[3], (D_MODEL,), 1.0),
        "w_ada": nrm(ks[4], (DEPTH, D_MODEL, 3 * D_MODEL), 0.5 * D_MODEL ** -0.5),
        "b_ada": nrm(ks[5], (DEPTH, 3 * D_MODEL), 0.02),
        "w_in": nrm(ks[6], (DEPTH, D_MODEL, IN_COLS), D_MODEL ** -0.5),
        "gqa_q_g": gain(ks[7], (DEPTH, GQA_HEAD_DIM)),
        "gqa_k_g": gain(ks[8], (DEPTH, GQA_HEAD_DIM)),
        "diff_lq1": nrm(ks[9], (DEPTH, DIFF_QK_DIM), 0.1),
        "diff_lk1": nrm(ks[10], (DEPTH, DIFF_QK_DIM), 0.1),
        "diff_lq2": nrm(ks[11], (DEPTH, DIFF_QK_DIM), 0.1),
        "diff_lk2": nrm(ks[12], (DEPTH, DIFF_QK_DIM), 0.1),
        "diff_subln_g": gain(ks[13], (DEPTH, DIFF_V_DIM)),
        "mla_q_g": gain(ks[14], (DEPTH, MLA_Q_RANK)),
        "w_uq": nrm(ks[15], (DEPTH, MLA_Q_RANK, MLA_HEADS * (MLA_NOPE + MLA_ROPE)), MLA_Q_RANK ** -0.5),
        "mla_kv_g": gain(ks[16], (DEPTH, MLA_KV_RANK)),
        "w_ukv": nrm(ks[17], (DEPTH, MLA_KV_RANK, MLA_HEADS * (MLA_NOPE + MLA_V)), MLA_KV_RANK ** -0.5),
        "w_out": nrm(ks[18], (DEPTH, MIX_WIDTH, D_MODEL), DEEPNORM_BETA * MIX_WIDTH ** -0.5),
        "ln_g": gain(ks[19], (DEPTH, D_MODEL)),
        "ln_b": nrm(ks[20], (DEPTH, D_MODEL), 0.02),
    }


def reference(x, c, ctx, c_ctx, w_ada, b_ada, w_in, gqa_q_g, gqa_k_g, diff_lq1, diff_lk1, diff_lq2, diff_lk2,
              diff_subln_g, mla_q_g, w_uq, mla_kv_g, w_ukv, w_out, ln_g, ln_b):
    S = x.shape[1]
    ROWS = S // GRID_W
    row = jnp.repeat(jnp.arange(ROWS, dtype=jnp.int32), GRID_W)
    col = jnp.tile(jnp.arange(GRID_W, dtype=jnp.int32), ROWS)
    tabs64 = axial_tables(row, col, GQA_HEAD_DIM, x.dtype)
    tabs32 = axial_tables(row, col, DIFF_QK_DIM, x.dtype)

    xl, xc = x, ctx
    for i in range(DEPTH):
        p = {
            "w_ada": w_ada[i], "b_ada": b_ada[i], "w_in": w_in[i],
            "gqa_q_g": gqa_q_g[i], "gqa_k_g": gqa_k_g[i],
            "diff_lq1": diff_lq1[i], "diff_lk1": diff_lk1[i], "diff_lq2": diff_lq2[i], "diff_lk2": diff_lk2[i],
            "diff_subln_g": diff_subln_g[i],
            "mla_q_g": mla_q_g[i], "w_uq": w_uq[i], "mla_kv_g": mla_kv_g[i], "w_ukv": w_ukv[i],
            "w_out": w_out[i], "ln_g": ln_g[i], "ln_b": ln_b[i],
        }
        lam_init = 0.8 - 0.6 * math.exp(-0.3 * i)
        xl, xc = hybrid_layer(xl, xc, c, c_ctx, p, lam_init, tabs64, tabs32, i < DEPTH - 1)
    return xl
```

```python
import functools
import math

import numpy as np
import jax
import jax.numpy as jnp
from jax import lax
from jax.experimental import pallas as pl
from jax.experimental.pallas import tpu as pltpu

D_MODEL = 1024
DEPTH = 4
GRID_W = 64
CTX_LEN = 256
ROPE_THETA = 10000.0
EPS = 1e-6

GQA_HEADS = 6
GQA_KV_HEADS = 2
GQA_HEAD_DIM = 64
DIFF_HEADS = 4
DIFF_QK_DIM = 32
DIFF_V_DIM = 64
MLA_HEADS = 6
MLA_NOPE = 64
MLA_ROPE = 32
MLA_V = 64
MLA_Q_RANK = 192
MLA_KV_RANK = 128
MIX_WIDTH = 1024

DEEPNORM_ALPHA = (2.0 * DEPTH) ** 0.25
GQA_SCALE = GQA_HEAD_DIM ** -0.5
DIFF_SCALE = DIFF_QK_DIM ** -0.5
MLA_SCALE = (MLA_NOPE + MLA_ROPE) ** -0.5

LANES = 128
ROW_BLOCK = 256
ADA_ROWS = 16
CTX_ROW = 8
ADA_COL_BLOCK = 512

OFF_GQ, OFF_GK, OFF_GV = 0, 384, 512
OFF_DQ, OFF_DK, OFF_DV = 640, 896, 1152
OFF_GATE = 1408
OFF_MKV = 2432
OFF_MQ = 2560
OFF_MKR = 2752
IN_COLS_P = 2816

Q_OFF_A, Q_OFF_D, Q_OFF_M = 0, 384, 640
Q_COLS = 1408
KT_OFF_A, KT_OFF_D, KT_OFF_M = 0, 128, 384
KT_ROWS = 1152
V_CH_A, V_CH_D, V_CH_M = 0, 2, 6
V_COLS = 12 * LANES

VMEM_LIMIT = 56 * 1024 * 1024


def _lane_iota(rows, cols=LANES):
    return lax.broadcasted_iota(jnp.int32, (rows, cols), 1)


def _swap_halves(x, half):
    lane = _lane_iota(x.shape[0])
    fwd = pltpu.roll(x, LANES - half, 1)
    bwd = pltpu.roll(x, half, 1)
    return jnp.where((lane % (2 * half)) < half, fwd, bwd)


def _rope(x, cos, sin, half):
    return x * cos + _swap_halves(x, half) * sin


def _ada_kernel(cc_ref, w_ref, b_ref, o_ref):
    cc = cc_ref[...]
    a = (cc * jax.nn.sigmoid(cc)).astype(jnp.bfloat16)
    w = w_ref[...].astype(jnp.bfloat16)
    o_ref[...] = jnp.dot(a, w, preferred_element_type=jnp.float32) + b_ref[...]


def _ada_call(cc, w_ada, b_ada3):
    n_col = (3 * D_MODEL) // ADA_COL_BLOCK
    return pl.pallas_call(
        _ada_kernel,
        grid=(DEPTH, n_col),
        in_specs=[
            pl.BlockSpec((ADA_ROWS, D_MODEL), lambda l, j: (0, 0)),
            pl.BlockSpec((None, D_MODEL, ADA_COL_BLOCK), lambda l, j: (l, 0, j)),
            pl.BlockSpec((None, 1, ADA_COL_BLOCK), lambda l, j: (l, 0, j)),
        ],
        out_specs=pl.BlockSpec((None, ADA_ROWS, ADA_COL_BLOCK), lambda l, j: (l, 0, j)),
        out_shape=jax.ShapeDtypeStruct((DEPTH, ADA_ROWS, 3 * D_MODEL), jnp.float32),
        name="ada_modulation",
    )(cc, w_ada, b_ada3)


def _proj_kernel(x_ref, mod_ref, win_ref, gains_ref, wuq_ref, wukv_ref, cos_ref, sin_ref,
                 q_ref, kt_ref, v_ref, g_ref):
    rows = x_ref.shape[0]
    lane = _lane_iota(rows)
    lo = lane < 64
    one_at_64 = (lane == 64).astype(jnp.float32)

    x = x_ref[...]
    mu = jnp.mean(x, axis=-1, keepdims=True)
    xc = x - mu
    var = jnp.mean(xc * xc, axis=-1, keepdims=True)
    mod = mod_ref[...]
    shift = mod[:, 0:D_MODEL]
    scale = mod[:, D_MODEL:2 * D_MODEL]
    h = (xc * lax.rsqrt(var + EPS) * (1.0 + scale) + shift).astype(jnp.bfloat16)
    z = jnp.dot(h, win_ref[...], preferred_element_type=jnp.float32)

    gains = gains_ref[...]
    g_gq = gains[0:1, 0:LANES]
    g_gk = gains[1:2, 0:LANES]
    g_mq = gains[2:3, :]
    g_mkv = gains[3:4, 0:LANES]
    cos_a, sin_a = cos_ref[0], sin_ref[0]
    cos_d, sin_d = cos_ref[1], sin_ref[1]
    cos_m, sin_m = cos_ref[2], sin_ref[2]

    def head_rms(zc, gain):
        sq = zc * zc
        s_lo = jnp.sum(jnp.where(lo, sq, 0.0), axis=-1, keepdims=True)
        s_hi = jnp.sum(jnp.where(lo, 0.0, sq), axis=-1, keepdims=True)
        r = jnp.where(lo, lax.rsqrt(s_lo / GQA_HEAD_DIM + EPS), lax.rsqrt(s_hi / GQA_HEAD_DIM + EPS))
        return zc * r * gain

    for c in range(3):
        zc = z[:, OFF_GQ + c * LANES:OFF_GQ + (c + 1) * LANES]
        y = _rope(head_rms(zc, g_gq), cos_a, sin_a, 16) * GQA_SCALE
        q_ref[:, Q_OFF_A + c * LANES:Q_OFF_A + (c + 1) * LANES] = y.astype(q_ref.dtype)
    yk = _rope(head_rms(z[:, OFF_GK:OFF_GK + LANES], g_gk), cos_a, sin_a, 16)
    kt_ref[KT_OFF_A:KT_OFF_A + LANES, :] = yk.T.astype(kt_ref.dtype)

    def store_v_pair(zv, chunk):
        v_lo = jnp.where(lo, zv, one_at_64)
        v_hi = jnp.where(lo, pltpu.roll(zv, 64, 1), one_at_64)
        v_ref[:, chunk * LANES:(chunk + 1) * LANES] = v_lo.astype(v_ref.dtype)
        v_ref[:, (chunk + 1) * LANES:(chunk + 2) * LANES] = v_hi.astype(v_ref.dtype)

    store_v_pair(z[:, OFF_GV:OFF_GV + LANES], V_CH_A)

    for c in range(2):
        zq = z[:, OFF_DQ + c * LANES:OFF_DQ + (c + 1) * LANES]
        yq = _rope(zq, cos_d, sin_d, 8) * DIFF_SCALE
        q_ref[:, Q_OFF_D + c * LANES:Q_OFF_D + (c + 1) * LANES] = yq.astype(q_ref.dtype)
        zk = z[:, OFF_DK + c * LANES:OFF_DK + (c + 1) * LANES]
        yk = _rope(zk, cos_d, sin_d, 8)
        kt_ref[KT_OFF_D + c * LANES:KT_OFF_D + (c + 1) * LANES, :] = yk.T.astype(kt_ref.dtype)
        store_v_pair(z[:, OFF_DV + c * LANES:OFF_DV + (c + 1) * LANES], V_CH_D + 2 * c)

    zg = z[:, OFF_GATE:OFF_GATE + MIX_WIDTH]
    g_ref[...] = (zg * jax.nn.sigmoid(zg)).astype(g_ref.dtype)

    ckv = z[:, OFF_MKV:OFF_MKV + MLA_KV_RANK]
    ms = jnp.mean(ckv * ckv, axis=-1, keepdims=True)
    cn = (ckv * lax.rsqrt(ms + EPS) * g_mkv).astype(jnp.bfloat16)
    kv = jnp.dot(cn, wukv_ref[...], preferred_element_type=jnp.float32)
    last = z[:, IN_COLS_P - LANES:IN_COLS_P]
    kpe = jnp.where((lane >= 64) & (lane < 96), last, 0.0)
    kpe = _rope(kpe, cos_m, sin_m, 8)
    n_m = MLA_HEADS * LANES
    for hh in range(MLA_HEADS):
        k_h = kv[:, hh * LANES:(hh + 1) * LANES] + kpe
        kt_ref[KT_OFF_M + hh * LANES:KT_OFF_M + (hh + 1) * LANES, :] = k_h.T.astype(kt_ref.dtype)
        v_h = kv[:, n_m + hh * LANES:n_m + (hh + 1) * LANES] + one_at_64
        v_ref[:, (V_CH_M + hh) * LANES:(V_CH_M + hh + 1) * LANES] = v_h.astype(v_ref.dtype)

    zq = z[:, OFF_MQ:OFF_MQ + 2 * LANES]
    lane2 = _lane_iota(rows, 2 * LANES)
    msq = jnp.sum(jnp.where(lane2 < MLA_Q_RANK, zq * zq, 0.0), axis=-1, keepdims=True) / MLA_Q_RANK
    qn = (zq * lax.rsqrt(msq + EPS) * g_mq).astype(jnp.bfloat16)
    qm = jnp.dot(qn, wuq_ref[...], preferred_element_type=jnp.float32)
    for hh in range(MLA_HEADS):
        yq = _rope(qm[:, hh * LANES:(hh + 1) * LANES], cos_m, sin_m, 8) * MLA_SCALE
        q_ref[:, Q_OFF_M + hh * LANES:Q_OFF_M + (hh + 1) * LANES] = yq.astype(q_ref.dtype)


def _proj_call(layer, xs, mod, w_in_p, gains, w_uq_p, w_ukv_p, cos_t, sin_t):
    B, T, _ = xs.shape
    nblk = T // ROW_BLOCK

    def mod_map(b, i):
        return (layer, jnp.where(i == 0, CTX_ROW, b), 0, 0)

    return pl.pallas_call(
        _proj_kernel,
        grid=(B, nblk),
        in_specs=[
            pl.BlockSpec((None, ROW_BLOCK, D_MODEL), lambda b, i: (b, i, 0)),
            pl.BlockSpec((None, None, 1, 3 * D_MODEL), mod_map),
            pl.BlockSpec((None, D_MODEL, IN_COLS_P), lambda b, i: (layer, 0, 0)),
            pl.BlockSpec((None, 8, 2 * LANES), lambda b, i: (layer, 0, 0)),
            pl.BlockSpec((None, 2 * LANES, MLA_HEADS * LANES), lambda b, i: (layer, 0, 0)),
            pl.BlockSpec((None, MLA_KV_RANK, 2 * MLA_HEADS * LANES), lambda b, i: (layer, 0, 0)),
            pl.BlockSpec((3, ROW_BLOCK, LANES), lambda b, i: (0, i, 0)),
            pl.BlockSpec((3, ROW_BLOCK, LANES), lambda b, i: (0, i, 0)),
        ],
        out_specs=[
            pl.BlockSpec((None, ROW_BLOCK, Q_COLS), lambda b, i: (b, i, 0)),
            pl.BlockSpec((None, KT_ROWS, ROW_BLOCK), lambda b, i: (b, 0, i)),
            pl.BlockSpec((None, ROW_BLOCK, V_COLS), lambda b, i: (b, i, 0)),
            pl.BlockSpec((None, ROW_BLOCK, MIX_WIDTH), lambda b, i: (b, i, 0)),
        ],
        out_shape=[
            jax.ShapeDtypeStruct((B, T, Q_COLS), jnp.bfloat16),
            jax.ShapeDtypeStruct((B, KT_ROWS, T), jnp.bfloat16),
            jax.ShapeDtypeStruct((B, T, V_COLS), jnp.bfloat16),
            jax.ShapeDtypeStruct((B, T, MIX_WIDTH), jnp.bfloat16),
        ],
        compiler_params=pltpu.CompilerParams(
            dimension_semantics=("arbitrary", "arbitrary"), vmem_limit_bytes=VMEM_LIMIT),
        name=f"proj_l{layer}",
    )(xs, mod, w_in_p, gains, w_uq_p, w_ukv_p, cos_t, sin_t)


def _attend(qm, kt, vext):
    s = jnp.dot(qm, kt, preferred_element_type=jnp.float32)
    m = jnp.max(s, axis=-1, keepdims=True)
    p = jnp.exp(s - m).astype(jnp.bfloat16)
    o = jnp.dot(p, vext, preferred_element_type=jnp.float32)
    return o / o[:, 64:65]


def _attn_kernel(q_ref, kt_ref, v_ref, g_ref, x_ref, mod_ref, wout_ref, vecs_ref, dl_ref, o_ref,
                 *, lam_init, ctx_queries):
    rows = x_ref.shape[0]
    lane = _lane_iota(rows)
    lo = lane < 64
    zero = jnp.zeros((), jnp.bfloat16)

    dl = dl_ref[...]
    lam = (jnp.exp(jnp.sum(dl[0:1] * dl[1:2], axis=-1, keepdims=True))
           - jnp.exp(jnp.sum(dl[2:3] * dl[3:4], axis=-1, keepdims=True)) + lam_init)
    vecs = vecs_ref[...]
    ln_g, ln_b = vecs[0:1], vecs[1:2]
    subln = vecs[2:3, 0:LANES] * (1.0 - lam_init)

    def body(keys):
        def kt_chunk(off):
            return kt_ref[off:off + LANES, 0:keys]

        def v_chunk(ch):
            return v_ref[0:keys, ch * LANES:(ch + 1) * LANES]

        heads = []
        kt_a = kt_chunk(KT_OFF_A)
        for hd in range(GQA_HEADS):
            c, half = hd % 3, hd // 3
            qc = q_ref[:, Q_OFF_A + c * LANES:Q_OFF_A + (c + 1) * LANES]
            qm = jnp.where(lo if half == 0 else ~lo, qc, zero)
            heads.append(_attend(qm, kt_a, v_chunk(V_CH_A + half)))
        for hd in range(DIFF_HEADS):
            c = hd // 2
            qc = q_ref[:, Q_OFF_D + c * LANES:Q_OFF_D + (c + 1) * LANES]
            kt_d = kt_chunk(KT_OFF_D + c * LANES)
            vv = v_chunk(V_CH_D + hd)
            outs = []
            for j in (2 * (hd % 2), 2 * (hd % 2) + 1):
                sel = (lane >= 32 * j) & (lane < 32 * (j + 1))
                outs.append(_attend(jnp.where(sel, qc, zero), kt_d, vv))
            od = outs[0] - lam * outs[1]
            ms = jnp.sum(jnp.where(lo, od * od, 0.0), axis=-1, keepdims=True) / DIFF_V_DIM
            heads.append(od * lax.rsqrt(ms + EPS) * subln)
        for hd in range(MLA_HEADS):
            qc = q_ref[:, Q_OFF_M + hd * LANES:Q_OFF_M + (hd + 1) * LANES]
            heads.append(_attend(qc, kt_chunk(KT_OFF_M + hd * LANES), v_chunk(V_CH_M + hd)))

        chunks = [jnp.where(lo, heads[2 * k], pltpu.roll(heads[2 * k + 1], 64, 1))
                  for k in range(MIX_WIDTH // LANES)]
        hmix = jnp.concatenate(chunks, axis=-1) * g_ref[...].astype(jnp.float32)
        y = jnp.dot(hmix.astype(jnp.bfloat16), wout_ref[...], preferred_element_type=jnp.float32)
        gate = mod_ref[...][:, 2 * D_MODEL:3 * D_MODEL]
        r = DEEPNORM_ALPHA * x_ref[...] + gate * y
        mu = jnp.mean(r, axis=-1, keepdims=True)
        rc = r - mu
        var = jnp.mean(rc * rc, axis=-1, keepdims=True)
        o_ref[...] = rc * lax.rsqrt(var + EPS) * ln_g + ln_b

    keys_all = kt_ref.shape[1]
    if ctx_queries:
        qi = pl.program_id(1)

        @pl.when(qi == 0)
        def _():
            body(CTX_LEN)

        @pl.when(qi > 0)
        def _():
            body(keys_all)
    else:
        body(keys_all)


def _attn_call(layer, q, kt, v, g, xs, mod, w_out_b, vecs, dl, lam_init, ctx_queries):
    B, T, _ = xs.shape
    nblk = T // ROW_BLOCK
    first = 0 if ctx_queries else 1

    def mod_map(b, i):
        if ctx_queries:
            return (layer, jnp.where(i == 0, CTX_ROW, b), 0, 0)
        return (layer, b, 0, 0)

    def rows_map(b, i):
        return (b, i + first, 0)

    kernel = functools.partial(_attn_kernel, lam_init=lam_init, ctx_queries=ctx_queries)
    return pl.pallas_call(
        kernel,
        grid=(B, nblk - first),
        in_specs=[
            pl.BlockSpec((None, ROW_BLOCK, Q_COLS), rows_map),
            pl.BlockSpec((None, KT_ROWS, T), lambda b, i: (b, 0, 0), pipeline_mode=pl.Buffered(1)),
            pl.BlockSpec((None, T, V_COLS), lambda b, i: (b, 0, 0), pipeline_mode=pl.Buffered(1)),
            pl.BlockSpec((None, ROW_BLOCK, MIX_WIDTH), rows_map),
            pl.BlockSpec((None, ROW_BLOCK, D_MODEL), rows_map),
            pl.BlockSpec((None, None, 1, 3 * D_MODEL), mod_map),
            pl.BlockSpec((None, MIX_WIDTH, D_MODEL), lambda b, i: (layer, 0, 0)),
            pl.BlockSpec((None, 8, D_MODEL), lambda b, i: (layer, 0, 0)),
            pl.BlockSpec((None, 4, DIFF_QK_DIM), lambda b, i: (layer, 0, 0)),
        ],
        out_specs=pl.BlockSpec((None, ROW_BLOCK, D_MODEL), lambda b, i: (b, i, 0)),
        out_shape=jax.ShapeDtypeStruct((B, T - first * ROW_BLOCK, D_MODEL), jnp.float32),
        compiler_params=pltpu.CompilerParams(
            dimension_semantics=("arbitrary", "arbitrary"), vmem_limit_bytes=VMEM_LIMIT),
        name=f"attn_l{layer}",
    )(q, kt, v, g, xs, mod, w_out_b, vecs, dl)


def _rope_tables(seq):
    rows_n = seq // GRID_W
    row = jnp.repeat(jnp.arange(rows_n, dtype=jnp.int32), GRID_W)
    col = jnp.tile(jnp.arange(GRID_W, dtype=jnp.int32), rows_n)

    def tables(pos, dim):
        freqs = ROPE_THETA ** (-jnp.arange(0, dim, 2, dtype=jnp.float32) / dim)
        ang = pos.astype(jnp.float32)[:, None] * freqs[None, :]
        return jnp.cos(ang), jnp.sin(ang)

    def head_tables(head_rot_dim):
        half = head_rot_dim // 2
        cr, sr = tables(row, half)
        cc, sc = tables(col, half)
        return (jnp.concatenate([cr, cr, cc, cc], axis=-1),
                jnp.concatenate([-sr, sr, -sc, sc], axis=-1))

    c64, s64 = head_tables(GQA_HEAD_DIM)
    c32, s32 = head_tables(DIFF_QK_DIM)
    ones, zeros = jnp.ones((seq, 64), jnp.float32), jnp.zeros((seq, 64), jnp.float32)
    cos_l = jnp.stack([jnp.tile(c64, (1, 2)), jnp.tile(c32, (1, 4)),
                       jnp.concatenate([ones, c32, ones[:, :32]], axis=-1)])
    sin_l = jnp.stack([jnp.tile(s64, (1, 2)), jnp.tile(s32, (1, 4)),
                       jnp.concatenate([zeros, s32, zeros[:, :32]], axis=-1)])
    cos_t = jnp.concatenate([jnp.ones((3, CTX_LEN, LANES), jnp.float32), cos_l], axis=1)
    sin_t = jnp.concatenate([jnp.zeros((3, CTX_LEN, LANES), jnp.float32), sin_l], axis=1)
    return cos_t, sin_t


def _in_proj_columns():
    sizes = (384, 128, 128, 256, 256, 256, 192, 128, 32, 384, 256, 384)
    starts = np.concatenate([[0], np.cumsum(sizes)[:-1]])
    (s_gq, s_gk, s_gv, s_dq, s_dk, s_dv, s_mq, s_mkv, s_mkr, s_ga, _, _) = [int(s) for s in starts]
    cols = np.full((IN_COLS_P,), -1, np.int64)
    for c in range(3):
        cols[OFF_GQ + c * 128:OFF_GQ + c * 128 + 64] = s_gq + c * 64 + np.arange(64)
        cols[OFF_GQ + c * 128 + 64:OFF_GQ + (c + 1) * 128] = s_gq + (3 + c) * 64 + np.arange(64)
    cols[OFF_GK:OFF_GK + 128] = s_gk + np.arange(128)
    cols[OFF_GV:OFF_GV + 128] = s_gv + np.arange(128)
    cols[OFF_DQ:OFF_DQ + 256] = s_dq + np.arange(256)
    cols[OFF_DK:OFF_DK + 256] = s_dk + np.arange(256)
    cols[OFF_DV:OFF_DV + 256] = s_dv + np.arange(256)
    cols[OFF_GATE:OFF_GATE + 1024] = s_ga + np.arange(1024)
    cols[OFF_MKV:OFF_MKV + 128] = s_mkv + np.arange(128)
    cols[OFF_MQ:OFF_MQ + 192] = s_mq + np.arange(192)
    cols[OFF_MKR:OFF_MKR + 32] = s_mkr + np.arange(32)
    return cols


def _gather_cols(w, cols):
    taken = jnp.take(w, jnp.asarray(np.maximum(cols, 0)), axis=-1)
    return jnp.where(jnp.asarray(cols >= 0), taken, 0.0)


def kernel(x, c, ctx, c_ctx, w_ada, b_ada, w_in, gqa_q_g, gqa_k_g, diff_lq1, diff_lk1, diff_lq2, diff_lk2,
           diff_subln_g, mla_q_g, w_uq, mla_kv_g, w_ukv, w_out, ln_g, ln_b):
    B, S, D = x.shape
    f32, bf16 = jnp.float32, jnp.bfloat16

    w_in_p = _gather_cols(w_in, _in_proj_columns()).astype(bf16)
    uq_cols = np.full((MLA_HEADS * LANES,), -1, np.int64)
    ukv_cols = np.full((2 * MLA_HEADS * LANES,), -1, np.int64)
    for hh in range(MLA_HEADS):
        uq_cols[hh * LANES:hh * LANES + 96] = hh * 96 + np.arange(96)
        ukv_cols[hh * LANES:hh * LANES + 64] = hh * 128 + np.arange(64)
        ukv_cols[(MLA_HEADS + hh) * LANES:(MLA_HEADS + hh) * LANES + 64] = hh * 128 + 64 + np.arange(64)
    w_uq_p = jnp.pad(_gather_cols(w_uq, uq_cols), ((0, 0), (0, 2 * LANES - MLA_Q_RANK), (0, 0))).astype(bf16)
    w_ukv_p = _gather_cols(w_ukv, ukv_cols).astype(bf16)
    w_out_b = w_out.astype(bf16)

    def row(vec, width):
        return jnp.pad(vec, ((0, 0), (0, width - vec.shape[-1])))[:, None, :]

    gains = jnp.concatenate([
        row(jnp.tile(gqa_q_g, (1, 2)), 2 * LANES), row(jnp.tile(gqa_k_g, (1, 2)), 2 * LANES),
        row(mla_q_g, 2 * LANES), row(mla_kv_g, 2 * LANES),
        jnp.zeros((DEPTH, 4, 2 * LANES), f32)], axis=1)
    vecs = jnp.concatenate([
        row(ln_g, D), row(ln_b, D), row(jnp.tile(diff_subln_g, (1, 2)), D),
        jnp.zeros((DEPTH, 5, D), f32)], axis=1)
    dl = jnp.stack([diff_lq1, diff_lk1, diff_lq2, diff_lk2], axis=1)
    cos_t, sin_t = _rope_tables(S)

    cc = jnp.concatenate([c, c_ctx[None, :], jnp.zeros((ADA_ROWS - B - 1, D), f32)], axis=0)
    mod = _ada_call(cc, w_ada, b_ada[:, None, :])[:, :, None, :]

    xs = jnp.concatenate([ctx, x], axis=1)
    for layer in range(DEPTH):
        lam_init = 0.8 - 0.6 * math.exp(-0.3 * layer)
        q, kt, v, g = _proj_call(layer, xs, mod, w_in_p, gains, w_uq_p, w_ukv_p, cos_t, sin_t)
        xs = _attn_call(layer, q, kt, v, g, xs, mod, w_out_b, vecs, dl, lam_init, layer < DEPTH - 1)
    return xs
```

```python
import functools
import math

import numpy as np
import jax
import jax.numpy as jnp
from jax import lax
from jax.experimental import pallas as pl
from jax.experimental.pallas import tpu as pltpu

D_MODEL = 1024
DEPTH = 4
GRID_W = 64
CTX_LEN = 256
ROPE_THETA = 10000.0
EPS = 1e-6

GQA_HEADS = 6
GQA_KV_HEADS = 2
GQA_HEAD_DIM = 64
DIFF_HEADS = 4
DIFF_QK_DIM = 32
DIFF_V_DIM = 64
MLA_HEADS = 6
MLA_NOPE = 64
MLA_ROPE = 32
MLA_V = 64
MLA_Q_RANK = 192
MLA_KV_RANK = 128
MIX_WIDTH = 1024

DEEPNORM_ALPHA = (2.0 * DEPTH) ** 0.25
LOG2_E = math.log2(math.e)
GQA_QSCALE = GQA_HEAD_DIM ** -0.5 * LOG2_E
DIFF_QSCALE = DIFF_QK_DIM ** -0.5 * LOG2_E
MLA_QSCALE = (MLA_NOPE + MLA_ROPE) ** -0.5 * LOG2_E

LANES = 128
ROW_BLOCK = 256
ADA_ROWS = 16
CTX_ROW = 8
ADA_COL_BLOCK = 512

OFF_GQ, OFF_GK, OFF_GV = 0, 384, 512
OFF_DQ, OFF_DK, OFF_DV = 640, 896, 1152
OFF_GATE = 1408
OFF_MKV = 2432
OFF_MQ = 2560
OFF_MKR = 2752
IN_COLS_P = 2816

Q_OFF_A, Q_OFF_D, Q_OFF_M = 0, 384, 640
Q_COLS = 1408
KT_OFF_A, KT_OFF_D, KT_OFF_M = 0, 128, 384
KT_ROWS = 1152
V_CH_A, V_CH_D, V_CH_M = 0, 2, 6
V_COLS = 12 * LANES

VMEM_LIMIT = 56 * 1024 * 1024


def _lane_iota(rows, cols=LANES):
    return lax.broadcasted_iota(jnp.int32, (rows, cols), 1)


def _swap_halves(x, half):
    lane = _lane_iota(x.shape[0])
    fwd = pltpu.roll(x, LANES - half, 1)
    bwd = pltpu.roll(x, half, 1)
    return jnp.where((lane % (2 * half)) < half, fwd, bwd)


def _rope(x, cos, sin, half):
    return x * cos + _swap_halves(x, half) * sin


def _ada_kernel(cc_ref, w_ref, b_ref, o_ref):
    cc = cc_ref[...]
    a = (cc * jax.nn.sigmoid(cc)).astype(jnp.bfloat16)
    w = w_ref[...].astype(jnp.bfloat16)
    o_ref[...] = jnp.dot(a, w, preferred_element_type=jnp.float32) + b_ref[...]


def _ada_call(cc, w_ada, b_ada3):
    n_col = (3 * D_MODEL) // ADA_COL_BLOCK
    return pl.pallas_call(
        _ada_kernel,
        grid=(DEPTH, n_col),
        in_specs=[
            pl.BlockSpec((ADA_ROWS, D_MODEL), lambda l, j: (0, 0)),
            pl.BlockSpec((None, D_MODEL, ADA_COL_BLOCK), lambda l, j: (l, 0, j)),
            pl.BlockSpec((None, 1, ADA_COL_BLOCK), lambda l, j: (l, 0, j)),
        ],
        out_specs=pl.BlockSpec((None, ADA_ROWS, ADA_COL_BLOCK), lambda l, j: (l, 0, j)),
        out_shape=jax.ShapeDtypeStruct((DEPTH, ADA_ROWS, 3 * D_MODEL), jnp.float32),
        name="ada_modulation",
    )(cc, w_ada, b_ada3)


def _proj_kernel(x_ref, mod_ref, win_ref, gains_ref, wuq_ref, wukv_ref, cos_ref, sin_ref,
                 q_ref, kt_ref, v_ref, g_ref):
    rows = x_ref.shape[0]
    lane = _lane_iota(rows)
    lo = lane < 64
    one_at_64 = (lane == 64).astype(jnp.float32)

    x = x_ref[...]
    mu = jnp.mean(x, axis=-1, keepdims=True)
    xc = x - mu
    var = jnp.mean(xc * xc, axis=-1, keepdims=True)
    mod = mod_ref[...]
    shift = mod[:, 0:D_MODEL]
    scale = mod[:, D_MODEL:2 * D_MODEL]
    h = (xc * lax.rsqrt(var + EPS) * (1.0 + scale) + shift).astype(jnp.bfloat16)
    z = jnp.dot(h, win_ref[...], preferred_element_type=jnp.float32)

    gains = gains_ref[...]
    g_gq = gains[0:1, 0:LANES]
    g_gk = gains[1:2, 0:LANES]
    g_mq = gains[2:3, :]
    g_mkv = gains[3:4, 0:LANES]
    cos_a, sin_a = cos_ref[0], sin_ref[0]
    cos_d, sin_d = cos_ref[1], sin_ref[1]
    cos_m, sin_m = cos_ref[2], sin_ref[2]

    def head_rms(zc, gain):
        sq = zc * zc
        s_lo = jnp.sum(jnp.where(lo, sq, 0.0), axis=-1, keepdims=True)
        s_hi = jnp.sum(jnp.where(lo, 0.0, sq), axis=-1, keepdims=True)
        r = jnp.where(lo, lax.rsqrt(s_lo / GQA_HEAD_DIM + EPS), lax.rsqrt(s_hi / GQA_HEAD_DIM + EPS))
        return zc * r * gain

    for c in range(3):
        zc = z[:, OFF_GQ + c * LANES:OFF_GQ + (c + 1) * LANES]
        y = _rope(head_rms(zc, g_gq), cos_a, sin_a, 16) * GQA_QSCALE
        q_ref[:, Q_OFF_A + c * LANES:Q_OFF_A + (c + 1) * LANES] = y.astype(q_ref.dtype)
    yk = _rope(head_rms(z[:, OFF_GK:OFF_GK + LANES], g_gk), cos_a, sin_a, 16)
    kt_ref[KT_OFF_A:KT_OFF_A + LANES, :] = yk.T.astype(kt_ref.dtype)

    def store_v_pair(zv, chunk):
        v_lo = jnp.where(lo, zv, one_at_64)
        v_hi = jnp.where(lo, pltpu.roll(zv, 64, 1), one_at_64)
        v_ref[:, chunk * LANES:(chunk + 1) * LANES] = v_lo.astype(v_ref.dtype)
        v_ref[:, (chunk + 1) * LANES:(chunk + 2) * LANES] = v_hi.astype(v_ref.dtype)

    store_v_pair(z[:, OFF_GV:OFF_GV + LANES], V_CH_A)

    for c in range(2):
        zq = z[:, OFF_DQ + c * LANES:OFF_DQ + (c + 1) * LANES]
        yq = _rope(zq, cos_d, sin_d, 8) * DIFF_QSCALE
        q_ref[:, Q_OFF_D + c * LANES:Q_OFF_D + (c + 1) * LANES] = yq.astype(q_ref.dtype)
        zk = z[:, OFF_DK + c * LANES:OFF_DK + (c + 1) * LANES]
        yk = _rope(zk, cos_d, sin_d, 8)
        kt_ref[KT_OFF_D + c * LANES:KT_OFF_D + (c + 1) * LANES, :] = yk.T.astype(kt_ref.dtype)
        store_v_pair(z[:, OFF_DV + c * LANES:OFF_DV + (c + 1) * LANES], V_CH_D + 2 * c)

    zg = z[:, OFF_GATE:OFF_GATE + MIX_WIDTH]
    g_ref[...] = (zg * jax.nn.sigmoid(zg)).astype(g_ref.dtype)

    ckv = z[:, OFF_MKV:OFF_MKV + MLA_KV_RANK]
    ms = jnp.mean(ckv * ckv, axis=-1, keepdims=True)
    cn = (ckv * lax.rsqrt(ms + EPS) * g_mkv).astype(jnp.bfloat16)
    kv = jnp.dot(cn, wukv_ref[...], preferred_element_type=jnp.float32)
    last = z[:, IN_COLS_P - LANES:IN_COLS_P]
    kpe = jnp.where((lane >= 64) & (lane < 96), last, 0.0)
    kpe = _rope(kpe, cos_m, sin_m, 8)
    n_m = MLA_HEADS * LANES
    for hh in range(MLA_HEADS):
        k_h = kv[:, hh * LANES:(hh + 1) * LANES] + kpe
        kt_ref[KT_OFF_M + hh * LANES:KT_OFF_M + (hh + 1) * LANES, :] = k_h.T.astype(kt_ref.dtype)
        v_h = kv[:, n_m + hh * LANES:n_m + (hh + 1) * LANES] + one_at_64
        v_ref[:, (V_CH_M + hh) * LANES:(V_CH_M + hh + 1) * LANES] = v_h.astype(v_ref.dtype)

    zq = z[:, OFF_MQ:OFF_MQ + 2 * LANES]
    lane2 = _lane_iota(rows, 2 * LANES)
    msq = jnp.sum(jnp.where(lane2 < MLA_Q_RANK, zq * zq, 0.0), axis=-1, keepdims=True) / MLA_Q_RANK
    qn = (zq * lax.rsqrt(msq + EPS) * g_mq).astype(jnp.bfloat16)
    qm = jnp.dot(qn, wuq_ref[...], preferred_element_type=jnp.float32)
    for hh in range(MLA_HEADS):
        yq = _rope(qm[:, hh * LANES:(hh + 1) * LANES], cos_m, sin_m, 8) * MLA_QSCALE
        q_ref[:, Q_OFF_M + hh * LANES:Q_OFF_M + (hh + 1) * LANES] = yq.astype(q_ref.dtype)


def _proj_call(layer, xs, mod, w_in_p, gains, w_uq_p, w_ukv_p, cos_t, sin_t):
    B, T, _ = xs.shape
    nblk = T // ROW_BLOCK

    def mod_map(b, i):
        return (layer, jnp.where(i == 0, CTX_ROW, b), 0, 0)

    return pl.pallas_call(
        _proj_kernel,
        grid=(B, nblk),
        in_specs=[
            pl.BlockSpec((None, ROW_BLOCK, D_MODEL), lambda b, i: (b, i, 0)),
            pl.BlockSpec((None, None, 1, 3 * D_MODEL), mod_map),
            pl.BlockSpec((None, D_MODEL, IN_COLS_P), lambda b, i: (layer, 0, 0)),
            pl.BlockSpec((None, 8, 2 * LANES), lambda b, i: (layer, 0, 0)),
            pl.BlockSpec((None, 2 * LANES, MLA_HEADS * LANES), lambda b, i: (layer, 0, 0)),
            pl.BlockSpec((None, MLA_KV_RANK, 2 * MLA_HEADS * LANES), lambda b, i: (layer, 0, 0)),
            pl.BlockSpec((3, ROW_BLOCK, LANES), lambda b, i: (0, i, 0)),
            pl.BlockSpec((3, ROW_BLOCK, LANES), lambda b, i: (0, i, 0)),
        ],
        out_specs=[
            pl.BlockSpec((None, ROW_BLOCK, Q_COLS), lambda b, i: (b, i, 0)),
            pl.BlockSpec((None, KT_ROWS, ROW_BLOCK), lambda b, i: (b, 0, i)),
            pl.BlockSpec((None, ROW_BLOCK, V_COLS), lambda b, i: (b, i, 0)),
            pl.BlockSpec((None, ROW_BLOCK, MIX_WIDTH), lambda b, i: (b, i, 0)),
        ],
        out_shape=[
            jax.ShapeDtypeStruct((B, T, Q_COLS), jnp.bfloat16),
            jax.ShapeDtypeStruct((B, KT_ROWS, T), jnp.bfloat16),
            jax.ShapeDtypeStruct((B, T, V_COLS), jnp.bfloat16),
            jax.ShapeDtypeStruct((B, T, MIX_WIDTH), jnp.bfloat16),
        ],
        compiler_params=pltpu.CompilerParams(
            dimension_semantics=("arbitrary", "arbitrary"), vmem_limit_bytes=VMEM_LIMIT),
        name=f"proj_l{layer}",
    )(xs, mod, w_in_p, gains, w_uq_p, w_ukv_p, cos_t, sin_t)


def _scores(qm, kt):
    return jnp.dot(qm, kt, preferred_element_type=jnp.float32)


def _probs(s):
    m = jnp.max(s, axis=-1, keepdims=True)
    return jnp.exp2(s - m).astype(jnp.bfloat16)


def _weighted_values(p, vext):
    o = jnp.dot(p, vext, preferred_element_type=jnp.float32)
    return o / o[:, 64:65]


def _attn_kernel(q_ref, kt_ref, v_ref, g_ref, x_ref, mod_ref, wout_ref, vecs_ref, dl_ref, o_ref,
                 *, lam_init, ctx_queries):
    rows = x_ref.shape[0]
    lane = _lane_iota(rows)
    lo = lane < 64
    zero = jnp.zeros((), jnp.bfloat16)

    dl = dl_ref[...]
    lam = (jnp.exp(jnp.sum(dl[0:1] * dl[1:2], axis=-1, keepdims=True))
           - jnp.exp(jnp.sum(dl[2:3] * dl[3:4], axis=-1, keepdims=True)) + lam_init)
    vecs = vecs_ref[...]
    ln_g, ln_b = vecs[0:1], vecs[1:2]
    subln = vecs[2:3, 0:LANES] * (1.0 - lam_init)

    maps = []
    for hd in range(GQA_HEADS):
        c, half = hd % 3, hd // 3
        maps.append((Q_OFF_A + c * LANES, (64 * half, 64 * half + 64), KT_OFF_A, V_CH_A + half))
    for hd in range(DIFF_HEADS):
        c = hd // 2
        for j in (2 * (hd % 2), 2 * (hd % 2) + 1):
            maps.append((Q_OFF_D + c * LANES, (32 * j, 32 * j + 32), KT_OFF_D + c * LANES, V_CH_D + hd))
    for hd in range(MLA_HEADS):
        maps.append((Q_OFF_M + hd * LANES, None, KT_OFF_M + hd * LANES, V_CH_M + hd))
    n_maps = len(maps)

    def body(keys):
        def scores(i):
            q_off, sel, kt_off, _ = maps[i]
            qc = q_ref[:, q_off:q_off + LANES]
            if sel is not None:
                qc = jnp.where((lane >= sel[0]) & (lane < sel[1]), qc, zero)
            return _scores(qc, kt_ref[kt_off:kt_off + LANES, 0:keys])

        def values(i, p):
            ch = maps[i][3]
            return _weighted_values(p, v_ref[0:keys, ch * LANES:(ch + 1) * LANES])

        outs = []
        p_cur = _probs(scores(0))
        s_next = scores(1)
        for i in range(n_maps):
            s_after = scores(i + 2) if i + 2 < n_maps else None
            p_next = _probs(s_next) if i + 1 < n_maps else None
            outs.append(values(i, p_cur))
            p_cur, s_next = p_next, s_after

        heads = outs[:GQA_HEADS]
        for hd in range(DIFF_HEADS):
            od = outs[GQA_HEADS + 2 * hd] - lam * outs[GQA_HEADS + 2 * hd + 1]
            ms = jnp.sum(jnp.where(lo, od * od, 0.0), axis=-1, keepdims=True) / DIFF_V_DIM
            heads.append(od * lax.rsqrt(ms + EPS) * subln)
        heads += outs[GQA_HEADS + 2 * DIFF_HEADS:]

        chunks = [jnp.where(lo, heads[2 * k], pltpu.roll(heads[2 * k + 1], 64, 1))
                  for k in range(MIX_WIDTH // LANES)]
        hmix = jnp.concatenate(chunks, axis=-1) * g_ref[...].astype(jnp.float32)
        y = jnp.dot(hmix.astype(jnp.bfloat16), wout_ref[...], preferred_element_type=jnp.float32)
        gate = mod_ref[...][:, 2 * D_MODEL:3 * D_MODEL]
        r = DEEPNORM_ALPHA * x_ref[...] + gate * y
        mu = jnp.mean(r, axis=-1, keepdims=True)
        rc = r - mu
        var = jnp.mean(rc * rc, axis=-1, keepdims=True)
        o_ref[...] = rc * lax.rsqrt(var + EPS) * ln_g + ln_b

    keys_all = kt_ref.shape[1]
    if ctx_queries:
        qi = pl.program_id(1)

        @pl.when(qi == 0)
        def _():
            body(CTX_LEN)

        @pl.when(qi > 0)
        def _():
            body(keys_all)
    else:
        body(keys_all)


def _attn_call(layer, q, kt, v, g, xs, mod, w_out_b, vecs, dl, lam_init, ctx_queries):
    B, T, _ = xs.shape
    nblk = T // ROW_BLOCK
    first = 0 if ctx_queries else 1

    def mod_map(b, i):
        if ctx_queries:
            return (layer, jnp.where(i == 0, CTX_ROW, b), 0, 0)
        return (layer, b, 0, 0)

    def rows_map(b, i):
        return (b, i + first, 0)

    kernel = functools.partial(_attn_kernel, lam_init=lam_init, ctx_queries=ctx_queries)
    return pl.pallas_call(
        kernel,
        grid=(B, nblk - first),
        in_specs=[
            pl.BlockSpec((None, ROW_BLOCK, Q_COLS), rows_map),
            pl.BlockSpec((None, KT_ROWS, T), lambda b, i: (b, 0, 0), pipeline_mode=pl.Buffered(1)),
            pl.BlockSpec((None, T, V_COLS), lambda b, i: (b, 0, 0), pipeline_mode=pl.Buffered(1)),
            pl.BlockSpec((None, ROW_BLOCK, MIX_WIDTH), rows_map),
            pl.BlockSpec((None, ROW_BLOCK, D_MODEL), rows_map),
            pl.BlockSpec((None, None, 1, 3 * D_MODEL), mod_map),
            pl.BlockSpec((None, MIX_WIDTH, D_MODEL), lambda b, i: (layer, 0, 0)),
            pl.BlockSpec((None, 8, D_MODEL), lambda b, i: (layer, 0, 0)),
            pl.BlockSpec((None, 4, DIFF_QK_DIM), lambda b, i: (layer, 0, 0)),
        ],
        out_specs=pl.BlockSpec((None, ROW_BLOCK, D_MODEL), lambda b, i: (b, i, 0)),
        out_shape=jax.ShapeDtypeStruct((B, T - first * ROW_BLOCK, D_MODEL), jnp.float32),
        compiler_params=pltpu.CompilerParams(
            dimension_semantics=("arbitrary", "arbitrary"), vmem_limit_bytes=VMEM_LIMIT),
        name=f"attn_l{layer}",
    )(q, kt, v, g, xs, mod, w_out_b, vecs, dl)


def _rope_tables(seq):
    rows_n = seq // GRID_W
    row = jnp.repeat(jnp.arange(rows_n, dtype=jnp.int32), GRID_W)
    col = jnp.tile(jnp.arange(GRID_W, dtype=jnp.int32), rows_n)

    def tables(pos, dim):
        freqs = ROPE_THETA ** (-jnp.arange(0, dim, 2, dtype=jnp.float32) / dim)
        ang = pos.astype(jnp.float32)[:, None] * freqs[None, :]
        return jnp.cos(ang), jnp.sin(ang)

    def head_tables(head_rot_dim):
        half = head_rot_dim // 2
        cr, sr = tables(row, half)
        cc, sc = tables(col, half)
        return (jnp.concatenate([cr, cr, cc, cc], axis=-1),
                jnp.concatenate([-sr, sr, -sc, sc], axis=-1))

    c64, s64 = head_tables(GQA_HEAD_DIM)
    c32, s32 = head_tables(DIFF_QK_DIM)
    ones, zeros = jnp.ones((seq, 64), jnp.float32), jnp.zeros((seq, 64), jnp.float32)
    cos_l = jnp.stack([jnp.tile(c64, (1, 2)), jnp.tile(c32, (1, 4)),
                       jnp.concatenate([ones, c32, ones[:, :32]], axis=-1)])
    sin_l = jnp.stack([jnp.tile(s64, (1, 2)), jnp.tile(s32, (1, 4)),
                       jnp.concatenate([zeros, s32, zeros[:, :32]], axis=-1)])
    cos_t = jnp.concatenate([jnp.ones((3, CTX_LEN, LANES), jnp.float32), cos_l], axis=1)
    sin_t = jnp.concatenate([jnp.zeros((3, CTX_LEN, LANES), jnp.float32), sin_l], axis=1)
    return cos_t, sin_t


def _in_proj_columns():
    sizes = (384, 128, 128, 256, 256, 256, 192, 128, 32, 384, 256, 384)
    starts = np.concatenate([[0], np.cumsum(sizes)[:-1]])
    (s_gq, s_gk, s_gv, s_dq, s_dk, s_dv, s_mq, s_mkv, s_mkr, s_ga, _, _) = [int(s) for s in starts]
    cols = np.full((IN_COLS_P,), -1, np.int64)
    for c in range(3):
        cols[OFF_GQ + c * 128:OFF_GQ + c * 128 + 64] = s_gq + c * 64 + np.arange(64)
        cols[OFF_GQ + c * 128 + 64:OFF_GQ + (c + 1) * 128] = s_gq + (3 + c) * 64 + np.arange(64)
    cols[OFF_GK:OFF_GK + 128] = s_gk + np.arange(128)
    cols[OFF_GV:OFF_GV + 128] = s_gv + np.arange(128)
    cols[OFF_DQ:OFF_DQ + 256] = s_dq + np.arange(256)
    cols[OFF_DK:OFF_DK + 256] = s_dk + np.arange(256)
    cols[OFF_DV:OFF_DV + 256] = s_dv + np.arange(256)
    cols[OFF_GATE:OFF_GATE + 1024] = s_ga + np.arange(1024)
    cols[OFF_MKV:OFF_MKV + 128] = s_mkv + np.arange(128)
    cols[OFF_MQ:OFF_MQ + 192] = s_mq + np.arange(192)
    cols[OFF_MKR:OFF_MKR + 32] = s_mkr + np.arange(32)
    return cols


def _gather_cols(w, cols):
    taken = jnp.take(w, jnp.asarray(np.maximum(cols, 0)), axis=-1)
    return jnp.where(jnp.asarray(cols >= 0), taken, 0.0)


def kernel(x, c, ctx, c_ctx, w_ada, b_ada, w_in, gqa_q_g, gqa_k_g, diff_lq1, diff_lk1, diff_lq2, diff_lk2,
           diff_subln_g, mla_q_g, w_uq, mla_kv_g, w_ukv, w_out, ln_g, ln_b):
    B, S, D = x.shape
    f32, bf16 = jnp.float32, jnp.bfloat16

    w_in_p = _gather_cols(w_in, _in_proj_columns()).astype(bf16)
    uq_cols = np.full((MLA_HEADS * LANES,), -1, np.int64)
    ukv_cols = np.full((2 * MLA_HEADS * LANES,), -1, np.int64)
    for hh in range(MLA_HEADS):
        uq_cols[hh * LANES:hh * LANES + 96] = hh * 96 + np.arange(96)
        ukv_cols[hh * LANES:hh * LANES + 64] = hh * 128 + np.arange(64)
        ukv_cols[(MLA_HEADS + hh) * LANES:(MLA_HEADS + hh) * LANES + 64] = hh * 128 + 64 + np.arange(64)
    w_uq_p = jnp.pad(_gather_cols(w_uq, uq_cols), ((0, 0), (0, 2 * LANES - MLA_Q_RANK), (0, 0))).astype(bf16)
    w_ukv_p = _gather_cols(w_ukv, ukv_cols).astype(bf16)
    w_out_b = w_out.astype(bf16)

    def row(vec, width):
        return jnp.pad(vec, ((0, 0), (0, width - vec.shape[-1])))[:, None, :]

    gains = jnp.concatenate([
        row(jnp.tile(gqa_q_g, (1, 2)), 2 * LANES), row(jnp.tile(gqa_k_g, (1, 2)), 2 * LANES),
        row(mla_q_g, 2 * LANES), row(mla_kv_g, 2 * LANES),
        jnp.zeros((DEPTH, 4, 2 * LANES), f32)], axis=1)
    vecs = jnp.concatenate([
        row(ln_g, D), row(ln_b, D), row(jnp.tile(diff_subln_g, (1, 2)), D),
        jnp.zeros((DEPTH, 5, D), f32)], axis=1)
    dl = jnp.stack([diff_lq1, diff_lk1, diff_lq2, diff_lk2], axis=1)
    cos_t, sin_t = _rope_tables(S)

    cc = jnp.concatenate([c, c_ctx[None, :], jnp.zeros((ADA_ROWS - B - 1, D), f32)], axis=0)
    mod = _ada_call(cc, w_ada, b_ada[:, None, :])[:, :, None, :]

    xs = jnp.concatenate([ctx, x], axis=1)
    for layer in range(DEPTH):
        lam_init = 0.8 - 0.6 * math.exp(-0.3 * layer)
        q, kt, v, g = _proj_call(layer, xs, mod, w_in_p, gains, w_uq_p, w_ukv_p, cos_t, sin_t)
        xs = _attn_call(layer, q, kt, v, g, xs, mod, w_out_b, vecs, dl, lam_init, layer < DEPTH - 1)
    return xs
```

```python
import functools
import math

import numpy as np
import jax
import jax.numpy as jnp
from jax import lax
from jax.experimental import pallas as pl
from jax.experimental.pallas import tpu as pltpu

D_MODEL = 1024
DEPTH = 4
GRID_W = 64
CTX_LEN = 256
ROPE_THETA = 10000.0
EPS = 1e-6

GQA_HEADS = 6
GQA_KV_HEADS = 2
GQA_HEAD_DIM = 64
DIFF_HEADS = 4
DIFF_QK_DIM = 32
DIFF_V_DIM = 64
MLA_HEADS = 6
MLA_NOPE = 64
MLA_ROPE = 32
MLA_V = 64
MLA_Q_RANK = 192
MLA_KV_RANK = 128
MIX_WIDTH = 1024

DEEPNORM_ALPHA = (2.0 * DEPTH) ** 0.25
LOG2_E = math.log2(math.e)
GQA_QSCALE = GQA_HEAD_DIM ** -0.5 * LOG2_E
DIFF_QSCALE = DIFF_QK_DIM ** -0.5 * LOG2_E
MLA_QSCALE = (MLA_NOPE + MLA_ROPE) ** -0.5 * LOG2_E

LANES = 128
ROW_BLOCK = 256
ATTN_ROWS = 256
ADA_ROWS = 16
CTX_ROW = 8
ADA_COL_BLOCK = 512

OFF_GQ, OFF_GK, OFF_GV = 0, 384, 512
OFF_DQ, OFF_DK, OFF_DV = 640, 896, 1152
OFF_GATE = 1408
OFF_MKV = 2432
OFF_MQ = 2560
OFF_MKR = 2752
IN_COLS_P = 2816

Q_OFF_A, Q_OFF_D, Q_OFF_M = 0, 384, 640
Q_COLS = 1408
KT_OFF_A, KT_OFF_D, KT_OFF_M = 0, 128, 384
KT_ROWS = 1152
V_CH_A, V_CH_D, V_CH_M = 0, 2, 6
V_COLS = 12 * LANES

VMEM_LIMIT = 56 * 1024 * 1024


def _lane_iota(rows, cols=LANES):
    return lax.broadcasted_iota(jnp.int32, (rows, cols), 1)


def _swap_halves(x, half):
    lane = _lane_iota(x.shape[0])
    fwd = pltpu.roll(x, LANES - half, 1)
    bwd = pltpu.roll(x, half, 1)
    return jnp.where((lane % (2 * half)) < half, fwd, bwd)


def _rope(x, cos, sin, half):
    return x * cos + _swap_halves(x, half) * sin


def _ada_kernel(cc_ref, w_ref, b_ref, o_ref):
    cc = cc_ref[...]
    a = (cc * jax.nn.sigmoid(cc)).astype(jnp.bfloat16)
    w = w_ref[...].astype(jnp.bfloat16)
    o_ref[...] = jnp.dot(a, w, preferred_element_type=jnp.float32) + b_ref[...]


def _ada_call(cc, w_ada, b_ada3):
    n_col = (3 * D_MODEL) // ADA_COL_BLOCK
    return pl.pallas_call(
        _ada_kernel,
        grid=(DEPTH, n_col),
        in_specs=[
            pl.BlockSpec((ADA_ROWS, D_MODEL), lambda l, j: (0, 0)),
            pl.BlockSpec((None, D_MODEL, ADA_COL_BLOCK), lambda l, j: (l, 0, j)),
            pl.BlockSpec((None, 1, ADA_COL_BLOCK), lambda l, j: (l, 0, j)),
        ],
        out_specs=pl.BlockSpec((None, ADA_ROWS, ADA_COL_BLOCK), lambda l, j: (l, 0, j)),
        out_shape=jax.ShapeDtypeStruct((DEPTH, ADA_ROWS, 3 * D_MODEL), jnp.float32),
        name="ada_modulation",
    )(cc, w_ada, b_ada3)


def _proj_kernel(xl_ref, xc_ref, mod_ref, win_ref, gains_ref, wuq_ref, wukv_ref, cos_ref, sin_ref,
                 q_ref, kt_ref, v_ref, g_ref):
    rows = xl_ref.shape[0]
    lane = _lane_iota(rows)
    lo = lane < 64
    one_at_64 = (lane == 64).astype(jnp.float32)

    is_ctx = pl.program_id(1) == pl.num_programs(1) - 1
    x = jnp.where(is_ctx, xc_ref[...], xl_ref[...])
    mu = jnp.mean(x, axis=-1, keepdims=True)
    xc = x - mu
    var = jnp.mean(xc * xc, axis=-1, keepdims=True)
    mod = mod_ref[...]
    shift = mod[:, 0:D_MODEL]
    scale = mod[:, D_MODEL:2 * D_MODEL]
    h = (xc * lax.rsqrt(var + EPS) * (1.0 + scale) + shift).astype(jnp.bfloat16)
    z = jnp.dot(h, win_ref[...], preferred_element_type=jnp.float32)

    gains = gains_ref[...]
    g_gq = gains[0:1, 0:LANES]
    g_gk = gains[1:2, 0:LANES]
    g_mq = gains[2:3, :]
    g_mkv = gains[3:4, 0:LANES]
    cos_a, sin_a = cos_ref[0], sin_ref[0]
    cos_d, sin_d = cos_ref[1], sin_ref[1]
    cos_m, sin_m = cos_ref[2], sin_ref[2]

    def head_rms(zc, gain):
        sq = zc * zc
        s_lo = jnp.sum(jnp.where(lo, sq, 0.0), axis=-1, keepdims=True)
        s_hi = jnp.sum(jnp.where(lo, 0.0, sq), axis=-1, keepdims=True)
        r = jnp.where(lo, lax.rsqrt(s_lo / GQA_HEAD_DIM + EPS), lax.rsqrt(s_hi / GQA_HEAD_DIM + EPS))
        return zc * r * gain

    for c in range(3):
        zc = z[:, OFF_GQ + c * LANES:OFF_GQ + (c + 1) * LANES]
        y = _rope(head_rms(zc, g_gq), cos_a, sin_a, 16) * GQA_QSCALE
        q_ref[:, Q_OFF_A + c * LANES:Q_OFF_A + (c + 1) * LANES] = y.astype(q_ref.dtype)
    yk = _rope(head_rms(z[:, OFF_GK:OFF_GK + LANES], g_gk), cos_a, sin_a, 16)
    kt_ref[KT_OFF_A:KT_OFF_A + LANES, :] = yk.T.astype(kt_ref.dtype)

    def store_v_pair(zv, chunk):
        v_lo = jnp.where(lo, zv, one_at_64)
        v_hi = jnp.where(lo, pltpu.roll(zv, 64, 1), one_at_64)
        v_ref[:, chunk * LANES:(chunk + 1) * LANES] = v_lo.astype(v_ref.dtype)
        v_ref[:, (chunk + 1) * LANES:(chunk + 2) * LANES] = v_hi.astype(v_ref.dtype)

    store_v_pair(z[:, OFF_GV:OFF_GV + LANES], V_CH_A)

    for c in range(2):
        zq = z[:, OFF_DQ + c * LANES:OFF_DQ + (c + 1) * LANES]
        yq = _rope(zq, cos_d, sin_d, 8) * DIFF_QSCALE
        q_ref[:, Q_OFF_D + c * LANES:Q_OFF_D + (c + 1) * LANES] = yq.astype(q_ref.dtype)
        zk = z[:, OFF_DK + c * LANES:OFF_DK + (c + 1) * LANES]
        yk = _rope(zk, cos_d, sin_d, 8)
        kt_ref[KT_OFF_D + c * LANES:KT_OFF_D + (c + 1) * LANES, :] = yk.T.astype(kt_ref.dtype)
        store_v_pair(z[:, OFF_DV + c * LANES:OFF_DV + (c + 1) * LANES], V_CH_D + 2 * c)

    zg = z[:, OFF_GATE:OFF_GATE + MIX_WIDTH]
    g_ref[...] = (zg * jax.nn.sigmoid(zg)).astype(g_ref.dtype)

    ckv = z[:, OFF_MKV:OFF_MKV + MLA_KV_RANK]
    ms = jnp.mean(ckv * ckv, axis=-1, keepdims=True)
    cn = (ckv * lax.rsqrt(ms + EPS) * g_mkv).astype(jnp.bfloat16)
    kv = jnp.dot(cn, wukv_ref[...], preferred_element_type=jnp.float32)
    last = z[:, IN_COLS_P - LANES:IN_COLS_P]
    kpe = jnp.where((lane >= 64) & (lane < 96), last, 0.0)
    kpe = _rope(kpe, cos_m, sin_m, 8)
    n_m = MLA_HEADS * LANES
    for hh in range(MLA_HEADS):
        k_h = kv[:, hh * LANES:(hh + 1) * LANES] + kpe
        kt_ref[KT_OFF_M + hh * LANES:KT_OFF_M + (hh + 1) * LANES, :] = k_h.T.astype(kt_ref.dtype)
        v_h = kv[:, n_m + hh * LANES:n_m + (hh + 1) * LANES] + one_at_64
        v_ref[:, (V_CH_M + hh) * LANES:(V_CH_M + hh + 1) * LANES] = v_h.astype(v_ref.dtype)

    zq = z[:, OFF_MQ:OFF_MQ + 2 * LANES]
    lane2 = _lane_iota(rows, 2 * LANES)
    msq = jnp.sum(jnp.where(lane2 < MLA_Q_RANK, zq * zq, 0.0), axis=-1, keepdims=True) / MLA_Q_RANK
    qn = (zq * lax.rsqrt(msq + EPS) * g_mq).astype(jnp.bfloat16)
    qm = jnp.dot(qn, wuq_ref[...], preferred_element_type=jnp.float32)
    for hh in range(MLA_HEADS):
        yq = _rope(qm[:, hh * LANES:(hh + 1) * LANES], cos_m, sin_m, 8) * MLA_QSCALE
        q_ref[:, Q_OFF_M + hh * LANES:Q_OFF_M + (hh + 1) * LANES] = yq.astype(q_ref.dtype)


def _proj_call(layer, xl, xc, mod, w_in_p, gains, w_uq_p, w_ukv_p, cos_t, sin_t):
    B, S, _ = xl.shape
    n_lat = S // ROW_BLOCK
    nblk = n_lat + 1
    T = S + CTX_LEN

    def mod_map(b, i):
        return (layer, jnp.where(i == n_lat, CTX_ROW, b), 0, 0)

    return pl.pallas_call(
        _proj_kernel,
        grid=(B, nblk),
        in_specs=[
            pl.BlockSpec((None, ROW_BLOCK, D_MODEL), lambda b, i: (b, jnp.minimum(i, n_lat - 1), 0)),
            pl.BlockSpec((None, CTX_LEN, D_MODEL), lambda b, i: (b, 0, 0)),
            pl.BlockSpec((None, None, 1, 3 * D_MODEL), mod_map),
            pl.BlockSpec((None, D_MODEL, IN_COLS_P), lambda b, i: (layer, 0, 0)),
            pl.BlockSpec((None, 8, 2 * LANES), lambda b, i: (layer, 0, 0)),
            pl.BlockSpec((None, 2 * LANES, MLA_HEADS * LANES), lambda b, i: (layer, 0, 0)),
            pl.BlockSpec((None, MLA_KV_RANK, 2 * MLA_HEADS * LANES), lambda b, i: (layer, 0, 0)),
            pl.BlockSpec((3, ROW_BLOCK, LANES), lambda b, i: (0, i, 0)),
            pl.BlockSpec((3, ROW_BLOCK, LANES), lambda b, i: (0, i, 0)),
        ],
        out_specs=[
            pl.BlockSpec((None, ROW_BLOCK, Q_COLS), lambda b, i: (b, i, 0)),
            pl.BlockSpec((None, KT_ROWS, ROW_BLOCK), lambda b, i: (b, 0, i)),
            pl.BlockSpec((None, ROW_BLOCK, V_COLS), lambda b, i: (b, i, 0)),
            pl.BlockSpec((None, ROW_BLOCK, MIX_WIDTH), lambda b, i: (b, i, 0)),
        ],
        out_shape=[
            jax.ShapeDtypeStruct((B, T, Q_COLS), jnp.bfloat16),
            jax.ShapeDtypeStruct((B, KT_ROWS, T), jnp.bfloat16),
            jax.ShapeDtypeStruct((B, T, V_COLS), jnp.bfloat16),
            jax.ShapeDtypeStruct((B, T, MIX_WIDTH), jnp.bfloat16),
        ],
        compiler_params=pltpu.CompilerParams(
            dimension_semantics=("arbitrary", "arbitrary"), vmem_limit_bytes=VMEM_LIMIT),
        name=f"proj_l{layer}",
    )(xl, xc, mod, w_in_p, gains, w_uq_p, w_ukv_p, cos_t, sin_t)


def _scores(qm, kt):
    return jnp.dot(qm, kt, preferred_element_type=jnp.float32)


def _probs(s):
    m = jnp.max(s, axis=-1, keepdims=True)
    return jnp.exp2(s - m).astype(jnp.bfloat16)


def _weighted_values(p, vext):
    o = jnp.dot(p, vext, preferred_element_type=jnp.float32)
    return o / o[:, 64:65]


def _attn_kernel(q_ref, kt_ref, v_ref, g_ref, x_ref, mod_ref, wout_ref, vecs_ref, dl_ref, o_ref,
                 *, lam_init, pipelined):
    rows = x_ref.shape[0]
    lane = _lane_iota(rows)
    lo = lane < 64
    zero = jnp.zeros((), jnp.bfloat16)

    dl = dl_ref[...]
    lam = (jnp.exp(jnp.sum(dl[0:1] * dl[1:2], axis=-1, keepdims=True))
           - jnp.exp(jnp.sum(dl[2:3] * dl[3:4], axis=-1, keepdims=True)) + lam_init)
    vecs = vecs_ref[...]
    ln_g, ln_b = vecs[0:1], vecs[1:2]
    subln = vecs[2:3, 0:LANES] * (1.0 - lam_init)

    maps = []
    for hd in range(GQA_HEADS):
        c, half = hd % 3, hd // 3
        maps.append((Q_OFF_A + c * LANES, (64 * half, 64 * half + 64), KT_OFF_A, V_CH_A + half))
    for hd in range(DIFF_HEADS):
        c = hd // 2
        for j in (2 * (hd % 2), 2 * (hd % 2) + 1):
            maps.append((Q_OFF_D + c * LANES, (32 * j, 32 * j + 32), KT_OFF_D + c * LANES, V_CH_D + hd))
    for hd in range(MLA_HEADS):
        maps.append((Q_OFF_M + hd * LANES, None, KT_OFF_M + hd * LANES, V_CH_M + hd))
    n_maps = len(maps)

    def scores(i):
        q_off, sel, kt_off, _ = maps[i]
        qc = q_ref[:, q_off:q_off + LANES]
        if sel is not None:
            qc = jnp.where((lane >= sel[0]) & (lane < sel[1]), qc, zero)
        return _scores(qc, kt_ref[kt_off:kt_off + LANES, :])

    def values(i, p):
        ch = maps[i][3]
        return _weighted_values(p, v_ref[:, ch * LANES:(ch + 1) * LANES])

    outs = []
    if pipelined:
        p_cur = _probs(scores(0))
        s_next = scores(1)
        for i in range(n_maps):
            s_after = scores(i + 2) if i + 2 < n_maps else None
            p_next = _probs(s_next) if i + 1 < n_maps else None
            outs.append(values(i, p_cur))
            p_cur, s_next = p_next, s_after
    else:
        all_p = [_probs(s) for s in [scores(i) for i in range(n_maps)]]
        outs = [values(i, p) for i, p in enumerate(all_p)]

    heads = outs[:GQA_HEADS]
    for hd in range(DIFF_HEADS):
        od = outs[GQA_HEADS + 2 * hd] - lam * outs[GQA_HEADS + 2 * hd + 1]
        ms = jnp.sum(jnp.where(lo, od * od, 0.0), axis=-1, keepdims=True) / DIFF_V_DIM
        heads.append(od * lax.rsqrt(ms + EPS) * subln)
    heads += outs[GQA_HEADS + 2 * DIFF_HEADS:]

    chunks = [jnp.where(lo, heads[2 * k], pltpu.roll(heads[2 * k + 1], 64, 1))
              for k in range(MIX_WIDTH // LANES)]
    hmix = jnp.concatenate(chunks, axis=-1) * g_ref[...].astype(jnp.float32)
    y = jnp.dot(hmix.astype(jnp.bfloat16), wout_ref[...], preferred_element_type=jnp.float32)
    gate = mod_ref[...][:, 2 * D_MODEL:3 * D_MODEL]
    r = DEEPNORM_ALPHA * x_ref[...] + gate * y
    mu = jnp.mean(r, axis=-1, keepdims=True)
    rc = r - mu
    var = jnp.mean(rc * rc, axis=-1, keepdims=True)
    o_ref[...] = rc * lax.rsqrt(var + EPS) * ln_g + ln_b


def _attn_latent_call(layer, q, kt, v, g, xl, mod, w_out_b, vecs, dl, lam_init):
    B, S, _ = xl.shape
    T = kt.shape[2]
    kernel = functools.partial(_attn_kernel, lam_init=lam_init, pipelined=True)
    return pl.pallas_call(
        kernel,
        grid=(B, S // ATTN_ROWS),
        in_specs=[
            pl.BlockSpec((None, ATTN_ROWS, Q_COLS), lambda b, i: (b, i, 0)),
            pl.BlockSpec((None, KT_ROWS, T), lambda b, i: (b, 0, 0), pipeline_mode=pl.Buffered(1)),
            pl.BlockSpec((None, T, V_COLS), lambda b, i: (b, 0, 0), pipeline_mode=pl.Buffered(1)),
            pl.BlockSpec((None, ATTN_ROWS, MIX_WIDTH), lambda b, i: (b, i, 0)),
            pl.BlockSpec((None, ATTN_ROWS, D_MODEL), lambda b, i: (b, i, 0)),
            pl.BlockSpec((None, None, 1, 3 * D_MODEL), lambda b, i: (layer, b, 0, 0)),
            pl.BlockSpec((None, MIX_WIDTH, D_MODEL), lambda b, i: (layer, 0, 0), pipeline_mode=pl.Buffered(1)),
            pl.BlockSpec((None, 8, D_MODEL), lambda b, i: (layer, 0, 0)),
            pl.BlockSpec((None, 4, DIFF_QK_DIM), lambda b, i: (layer, 0, 0)),
        ],
        out_specs=pl.BlockSpec((None, ATTN_ROWS, D_MODEL), lambda b, i: (b, i, 0)),
        out_shape=jax.ShapeDtypeStruct((B, S, D_MODEL), jnp.float32),
        compiler_params=pltpu.CompilerParams(
            dimension_semantics=("arbitrary", "arbitrary"), vmem_limit_bytes=VMEM_LIMIT),
        name=f"attn_latent_l{layer}",
    )(q, kt, v, g, xl, mod, w_out_b, vecs, dl)


def _attn_ctx_call(layer, q, kt, v, g, xc, mod, w_out_b, vecs, dl, lam_init):
    B = xc.shape[0]
    blk = (kt.shape[2] - CTX_LEN) // CTX_LEN
    kernel = functools.partial(_attn_kernel, lam_init=lam_init, pipelined=False)
    return pl.pallas_call(
        kernel,
        grid=(B,),
        in_specs=[
            pl.BlockSpec((None, CTX_LEN, Q_COLS), lambda b: (b, blk, 0)),
            pl.BlockSpec((None, KT_ROWS, CTX_LEN), lambda b: (b, 0, blk)),
            pl.BlockSpec((None, CTX_LEN, V_COLS), lambda b: (b, blk, 0)),
            pl.BlockSpec((None, CTX_LEN, MIX_WIDTH), lambda b: (b, blk, 0)),
            pl.BlockSpec((None, CTX_LEN, D_MODEL), lambda b: (b, 0, 0)),
            pl.BlockSpec((None, None, 1, 3 * D_MODEL), lambda b: (layer, CTX_ROW, 0, 0)),
            pl.BlockSpec((None, MIX_WIDTH, D_MODEL), lambda b: (layer, 0, 0)),
            pl.BlockSpec((None, 8, D_MODEL), lambda b: (layer, 0, 0)),
            pl.BlockSpec((None, 4, DIFF_QK_DIM), lambda b: (layer, 0, 0)),
        ],
        out_specs=pl.BlockSpec((None, CTX_LEN, D_MODEL), lambda b: (b, 0, 0)),
        out_shape=jax.ShapeDtypeStruct((B, CTX_LEN, D_MODEL), jnp.float32),
        compiler_params=pltpu.CompilerParams(
            dimension_semantics=("arbitrary",), vmem_limit_bytes=VMEM_LIMIT),
        name=f"attn_ctx_l{layer}",
    )(q, kt, v, g, xc, mod, w_out_b, vecs, dl)


def _rope_tables(seq):
    rows_n = seq // GRID_W
    row = jnp.repeat(jnp.arange(rows_n, dtype=jnp.int32), GRID_W)
    col = jnp.tile(jnp.arange(GRID_W, dtype=jnp.int32), rows_n)

    def tables(pos, dim):
        freqs = ROPE_THETA ** (-jnp.arange(0, dim, 2, dtype=jnp.float32) / dim)
        ang = pos.astype(jnp.float32)[:, None] * freqs[None, :]
        return jnp.cos(ang), jnp.sin(ang)

    def head_tables(head_rot_dim):
        half = head_rot_dim // 2
        cr, sr = tables(row, half)
        cc, sc = tables(col, half)
        return (jnp.concatenate([cr, cr, cc, cc], axis=-1),
                jnp.concatenate([-sr, sr, -sc, sc], axis=-1))

    c64, s64 = head_tables(GQA_HEAD_DIM)
    c32, s32 = head_tables(DIFF_QK_DIM)
    ones, zeros = jnp.ones((seq, 64), jnp.float32), jnp.zeros((seq, 64), jnp.float32)
    cos_l = jnp.stack([jnp.tile(c64, (1, 2)), jnp.tile(c32, (1, 4)),
                       jnp.concatenate([ones, c32, ones[:, :32]], axis=-1)])
    sin_l = jnp.stack([jnp.tile(s64, (1, 2)), jnp.tile(s32, (1, 4)),
                       jnp.concatenate([zeros, s32, zeros[:, :32]], axis=-1)])
    cos_t = jnp.concatenate([cos_l, jnp.ones((3, CTX_LEN, LANES), jnp.float32)], axis=1)
    sin_t = jnp.concatenate([sin_l, jnp.zeros((3, CTX_LEN, LANES), jnp.float32)], axis=1)
    return cos_t, sin_t


def _in_proj_columns():
    sizes = (384, 128, 128, 256, 256, 256, 192, 128, 32, 384, 256, 384)
    starts = np.concatenate([[0], np.cumsum(sizes)[:-1]])
    (s_gq, s_gk, s_gv, s_dq, s_dk, s_dv, s_mq, s_mkv, s_mkr, s_ga, _, _) = [int(s) for s in starts]
    cols = np.full((IN_COLS_P,), -1, np.int64)
    for c in range(3):
        cols[OFF_GQ + c * 128:OFF_GQ + c * 128 + 64] = s_gq + c * 64 + np.arange(64)
        cols[OFF_GQ + c * 128 + 64:OFF_GQ + (c + 1) * 128] = s_gq + (3 + c) * 64 + np.arange(64)
    cols[OFF_GK:OFF_GK + 128] = s_gk + np.arange(128)
    cols[OFF_GV:OFF_GV + 128] = s_gv + np.arange(128)
    cols[OFF_DQ:OFF_DQ + 256] = s_dq + np.arange(256)
    cols[OFF_DK:OFF_DK + 256] = s_dk + np.arange(256)
    cols[OFF_DV:OFF_DV + 256] = s_dv + np.arange(256)
    cols[OFF_GATE:OFF_GATE + 1024] = s_ga + np.arange(1024)
    cols[OFF_MKV:OFF_MKV + 128] = s_mkv + np.arange(128)
    cols[OFF_MQ:OFF_MQ + 192] = s_mq + np.arange(192)
    cols[OFF_MKR:OFF_MKR + 32] = s_mkr + np.arange(32)
    return cols


def _gather_cols(w, cols):
    parts, start = [], 0
    for j in range(1, len(cols) + 1):
        run_ends = (j == len(cols) or (cols[j] < 0) != (cols[start] < 0)
                    or (cols[start] >= 0 and cols[j] != cols[j - 1] + 1))
        if run_ends:
            if cols[start] < 0:
                parts.append(jnp.zeros(w.shape[:-1] + (j - start,), w.dtype))
            else:
                parts.append(w[..., int(cols[start]):int(cols[j - 1]) + 1])
            start = j
    return jnp.concatenate(parts, axis=-1)


def kernel(x, c, ctx, c_ctx, w_ada, b_ada, w_in, gqa_q_g, gqa_k_g, diff_lq1, diff_lk1, diff_lq2, diff_lk2,
           diff_subln_g, mla_q_g, w_uq, mla_kv_g, w_ukv, w_out, ln_g, ln_b):
    B, S, D = x.shape
    f32, bf16 = jnp.float32, jnp.bfloat16

    w_in_p = _gather_cols(w_in, _in_proj_columns()).astype(bf16)
    uq_cols = np.full((MLA_HEADS * LANES,), -1, np.int64)
    ukv_cols = np.full((2 * MLA_HEADS * LANES,), -1, np.int64)
    for hh in range(MLA_HEADS):
        uq_cols[hh * LANES:hh * LANES + 96] = hh * 96 + np.arange(96)
        ukv_cols[hh * LANES:hh * LANES + 64] = hh * 128 + np.arange(64)
        ukv_cols[(MLA_HEADS + hh) * LANES:(MLA_HEADS + hh) * LANES + 64] = hh * 128 + 64 + np.arange(64)
    w_uq_p = jnp.pad(_gather_cols(w_uq, uq_cols), ((0, 0), (0, 2 * LANES - MLA_Q_RANK), (0, 0))).astype(bf16)
    w_ukv_p = _gather_cols(w_ukv, ukv_cols).astype(bf16)
    w_out_b = w_out.astype(bf16)

    def row(vec, width):
        return jnp.pad(vec, ((0, 0), (0, width - vec.shape[-1])))[:, None, :]

    gains = jnp.concatenate([
        row(jnp.tile(gqa_q_g, (1, 2)), 2 * LANES), row(jnp.tile(gqa_k_g, (1, 2)), 2 * LANES),
        row(mla_q_g, 2 * LANES), row(mla_kv_g, 2 * LANES),
        jnp.zeros((DEPTH, 4, 2 * LANES), f32)], axis=1)
    vecs = jnp.concatenate([
        row(ln_g, D), row(ln_b, D), row(jnp.tile(diff_subln_g, (1, 2)), D),
        jnp.zeros((DEPTH, 5, D), f32)], axis=1)
    dl = jnp.stack([diff_lq1, diff_lk1, diff_lq2, diff_lk2], axis=1)
    cos_t, sin_t = _rope_tables(S)

    cc = jnp.concatenate([c, c_ctx[None, :], jnp.zeros((ADA_ROWS - B - 1, D), f32)], axis=0)
    mod = _ada_call(cc, w_ada, b_ada[:, None, :])[:, :, None, :]

    xl, xc = x, ctx
    for layer in range(DEPTH):
        lam_init = 0.8 - 0.6 * math.exp(-0.3 * layer)
        q, kt, v, g = _proj_call(layer, xl, xc, mod, w_in_p, gains, w_uq_p, w_ukv_p, cos_t, sin_t)
        xl_new = _attn_latent_call(layer, q, kt, v, g, xl, mod, w_out_b, vecs, dl, lam_init)
        if layer < DEPTH - 1:
            xc = _attn_ctx_call(layer, q, kt, v, g, xc, mod, w_out_b, vecs, dl, lam_init)
        xl = xl_new
    return xl
```

```python
import functools
import math

import numpy as np
import jax
import jax.numpy as jnp
from jax import lax
from jax.experimental import pallas as pl
from jax.experimental.pallas import tpu as pltpu

D_MODEL = 1024
DEPTH = 4
GRID_W = 64
CTX_LEN = 256
ROPE_THETA = 10000.0
EPS = 1e-6

GQA_HEADS = 6
GQA_KV_HEADS = 2
GQA_HEAD_DIM = 64
DIFF_HEADS = 4
DIFF_QK_DIM = 32
DIFF_V_DIM = 64
MLA_HEADS = 6
MLA_NOPE = 64
MLA_ROPE = 32
MLA_V = 64
MLA_Q_RANK = 192
MLA_KV_RANK = 128
MIX_WIDTH = 1024

DEEPNORM_ALPHA = (2.0 * DEPTH) ** 0.25
LOG2_E = math.log2(math.e)
GQA_QSCALE = GQA_HEAD_DIM ** -0.5 * LOG2_E
DIFF_QSCALE = DIFF_QK_DIM ** -0.5 * LOG2_E
MLA_QSCALE = (MLA_NOPE + MLA_ROPE) ** -0.5 * LOG2_E

LANES = 128
ROW_BLOCK = 256
ATTN_ROWS = 256
ADA_ROWS = 16
CTX_ROW = 8
ADA_COL_BLOCK = 512

OFF_GQ, OFF_GK, OFF_GV = 0, 384, 512
OFF_DQ, OFF_DK, OFF_DV = 640, 896, 1152
OFF_GATE = 1408
OFF_MKV = 2432
OFF_MQ = 2560
OFF_MKR = 2752
IN_COLS_P = 2816

Q_OFF_A, Q_OFF_D, Q_OFF_M = 0, 384, 640
Q_COLS = 1408
KT_OFF_A, KT_OFF_D, KT_OFF_M = 0, 128, 384
KT_ROWS = 1152
V_CH_A, V_CH_D, V_CH_M = 0, 2, 6
V_COLS = 12 * LANES

VMEM_LIMIT = 56 * 1024 * 1024


def _lane_iota(rows, cols=LANES):
    return lax.broadcasted_iota(jnp.int32, (rows, cols), 1)


def _swap_halves(x, half):
    lane = _lane_iota(x.shape[0])
    fwd = pltpu.roll(x, LANES - half, 1)
    bwd = pltpu.roll(x, half, 1)
    return jnp.where((lane % (2 * half)) < half, fwd, bwd)


def _rope(x, cos, sin, half):
    return x * cos + _swap_halves(x, half) * sin


def _ada_kernel(cc_ref, w_ref, b_ref, o_ref):
    cc = cc_ref[...]
    a = (cc * jax.nn.sigmoid(cc)).astype(jnp.bfloat16)
    w = w_ref[...].astype(jnp.bfloat16)
    o_ref[...] = jnp.dot(a, w, preferred_element_type=jnp.float32) + b_ref[...]


def _ada_call(cc, w_ada, b_ada3):
    n_col = (3 * D_MODEL) // ADA_COL_BLOCK
    return pl.pallas_call(
        _ada_kernel,
        grid=(DEPTH, n_col),
        in_specs=[
            pl.BlockSpec((ADA_ROWS, D_MODEL), lambda l, j: (0, 0)),
            pl.BlockSpec((None, D_MODEL, ADA_COL_BLOCK), lambda l, j: (l, 0, j)),
            pl.BlockSpec((None, 1, ADA_COL_BLOCK), lambda l, j: (l, 0, j)),
        ],
        out_specs=pl.BlockSpec((None, ADA_ROWS, ADA_COL_BLOCK), lambda l, j: (l, 0, j)),
        out_shape=jax.ShapeDtypeStruct((DEPTH, ADA_ROWS, 3 * D_MODEL), jnp.float32),
        name="ada_modulation",
    )(cc, w_ada, b_ada3)


def _proj_kernel(xl_ref, xc_ref, mod_ref, win_ref, gains_ref, wuq_ref, wukv_ref, cos_ref, sin_ref,
                 q_ref, kt_ref, v_ref, g_ref):
    rows = xl_ref.shape[0]
    lane = _lane_iota(rows)
    lo = lane < 64
    one_at_64 = (lane == 64).astype(jnp.float32)

    is_ctx = pl.program_id(1) == pl.num_programs(1) - 1
    x = jnp.where(is_ctx, xc_ref[...], xl_ref[...])
    mu = jnp.mean(x, axis=-1, keepdims=True)
    xc = x - mu
    var = jnp.mean(xc * xc, axis=-1, keepdims=True)
    mod = mod_ref[...]
    shift = mod[:, 0:D_MODEL]
    scale = mod[:, D_MODEL:2 * D_MODEL]
    h = (xc * lax.rsqrt(var + EPS) * (1.0 + scale) + shift).astype(jnp.bfloat16)
    z = jnp.dot(h, win_ref[...], preferred_element_type=jnp.float32)

    gains = gains_ref[...]
    g_gq = gains[0:1, 0:LANES]
    g_gk = gains[1:2, 0:LANES]
    g_mq = gains[2:3, :]
    g_mkv = gains[3:4, 0:LANES]
    cos_a, sin_a = cos_ref[0], sin_ref[0]
    cos_d, sin_d = cos_ref[1], sin_ref[1]
    cos_m, sin_m = cos_ref[2], sin_ref[2]

    def head_rms(zc, gain):
        sq = zc * zc
        s_lo = jnp.sum(jnp.where(lo, sq, 0.0), axis=-1, keepdims=True)
        s_hi = jnp.sum(jnp.where(lo, 0.0, sq), axis=-1, keepdims=True)
        r = jnp.where(lo, lax.rsqrt(s_lo / GQA_HEAD_DIM + EPS), lax.rsqrt(s_hi / GQA_HEAD_DIM + EPS))
        return zc * r * gain

    for c in range(3):
        zc = z[:, OFF_GQ + c * LANES:OFF_GQ + (c + 1) * LANES]
        y = _rope(head_rms(zc, g_gq), cos_a, sin_a, 16) * GQA_QSCALE
        q_ref[:, Q_OFF_A + c * LANES:Q_OFF_A + (c + 1) * LANES] = y.astype(q_ref.dtype)
    yk = _rope(head_rms(z[:, OFF_GK:OFF_GK + LANES], g_gk), cos_a, sin_a, 16)
    kt_ref[KT_OFF_A:KT_OFF_A + LANES, :] = yk.T.astype(kt_ref.dtype)

    def store_v_pair(zv, chunk):
        v_lo = jnp.where(lo, zv, one_at_64)
        v_hi = jnp.where(lo, pltpu.roll(zv, 64, 1), one_at_64)
        v_ref[:, chunk * LANES:(chunk + 1) * LANES] = v_lo.astype(v_ref.dtype)
        v_ref[:, (chunk + 1) * LANES:(chunk + 2) * LANES] = v_hi.astype(v_ref.dtype)

    store_v_pair(z[:, OFF_GV:OFF_GV + LANES], V_CH_A)

    for c in range(2):
        zq = z[:, OFF_DQ + c * LANES:OFF_DQ + (c + 1) * LANES]
        yq = _rope(zq, cos_d, sin_d, 8) * DIFF_QSCALE
        q_ref[:, Q_OFF_D + c * LANES:Q_OFF_D + (c + 1) * LANES] = yq.astype(q_ref.dtype)
        zk = z[:, OFF_DK + c * LANES:OFF_DK + (c + 1) * LANES]
        yk = _rope(zk, cos_d, sin_d, 8)
        kt_ref[KT_OFF_D + c * LANES:KT_OFF_D + (c + 1) * LANES, :] = yk.T.astype(kt_ref.dtype)
        store_v_pair(z[:, OFF_DV + c * LANES:OFF_DV + (c + 1) * LANES], V_CH_D + 2 * c)

    zg = z[:, OFF_GATE:OFF_GATE + MIX_WIDTH]
    g_ref[...] = (zg * jax.nn.sigmoid(zg)).astype(g_ref.dtype)

    ckv = z[:, OFF_MKV:OFF_MKV + MLA_KV_RANK]
    ms = jnp.mean(ckv * ckv, axis=-1, keepdims=True)
    cn = (ckv * lax.rsqrt(ms + EPS) * g_mkv).astype(jnp.bfloat16)
    kv = jnp.dot(cn, wukv_ref[...], preferred_element_type=jnp.float32)
    last = z[:, IN_COLS_P - LANES:IN_COLS_P]
    kpe = jnp.where((lane >= 64) & (lane < 96), last, 0.0)
    kpe = _rope(kpe, cos_m, sin_m, 8)
    n_m = MLA_HEADS * LANES
    for hh in range(MLA_HEADS):
        k_h = kv[:, hh * LANES:(hh + 1) * LANES] + kpe
        kt_ref[KT_OFF_M + hh * LANES:KT_OFF_M + (hh + 1) * LANES, :] = k_h.T.astype(kt_ref.dtype)
        v_h = kv[:, n_m + hh * LANES:n_m + (hh + 1) * LANES] + one_at_64
        v_ref[:, (V_CH_M + hh) * LANES:(V_CH_M + hh + 1) * LANES] = v_h.astype(v_ref.dtype)

    zq = z[:, OFF_MQ:OFF_MQ + 2 * LANES]
    lane2 = _lane_iota(rows, 2 * LANES)
    msq = jnp.sum(jnp.where(lane2 < MLA_Q_RANK, zq * zq, 0.0), axis=-1, keepdims=True) / MLA_Q_RANK
    qn = (zq * lax.rsqrt(msq + EPS) * g_mq).astype(jnp.bfloat16)
    qm = jnp.dot(qn, wuq_ref[...], preferred_element_type=jnp.float32)
    for hh in range(MLA_HEADS):
        yq = _rope(qm[:, hh * LANES:(hh + 1) * LANES], cos_m, sin_m, 8) * MLA_QSCALE
        q_ref[:, Q_OFF_M + hh * LANES:Q_OFF_M + (hh + 1) * LANES] = yq.astype(q_ref.dtype)


def _proj_call(layer, xl, xc, mod, w_in_p, gains, w_uq_p, w_ukv_p, cos_t, sin_t):
    B, S, _ = xl.shape
    n_lat = S // ROW_BLOCK
    nblk = n_lat + 1
    T = S + CTX_LEN

    def mod_map(b, i):
        return (layer, jnp.where(i == n_lat, CTX_ROW, b), 0, 0)

    return pl.pallas_call(
        _proj_kernel,
        grid=(B, nblk),
        in_specs=[
            pl.BlockSpec((None, ROW_BLOCK, D_MODEL), lambda b, i: (b, jnp.minimum(i, n_lat - 1), 0)),
            pl.BlockSpec((None, CTX_LEN, D_MODEL), lambda b, i: (b, 0, 0)),
            pl.BlockSpec((None, None, 1, 3 * D_MODEL), mod_map),
            pl.BlockSpec((None, D_MODEL, IN_COLS_P), lambda b, i: (layer, 0, 0)),
            pl.BlockSpec((None, 8, 2 * LANES), lambda b, i: (layer, 0, 0)),
            pl.BlockSpec((None, 2 * LANES, MLA_HEADS * LANES), lambda b, i: (layer, 0, 0)),
            pl.BlockSpec((None, MLA_KV_RANK, 2 * MLA_HEADS * LANES), lambda b, i: (layer, 0, 0)),
            pl.BlockSpec((3, ROW_BLOCK, LANES), lambda b, i: (0, i, 0)),
            pl.BlockSpec((3, ROW_BLOCK, LANES), lambda b, i: (0, i, 0)),
        ],
        out_specs=[
            pl.BlockSpec((None, ROW_BLOCK, Q_COLS), lambda b, i: (b, i, 0)),
            pl.BlockSpec((None, KT_ROWS, ROW_BLOCK), lambda b, i: (b, 0, i)),
            pl.BlockSpec((None, ROW_BLOCK, V_COLS), lambda b, i: (b, i, 0)),
            pl.BlockSpec((None, ROW_BLOCK, MIX_WIDTH), lambda b, i: (b, i, 0)),
        ],
        out_shape=[
            jax.ShapeDtypeStruct((B, T, Q_COLS), jnp.bfloat16),
            jax.ShapeDtypeStruct((B, KT_ROWS, T), jnp.bfloat16),
            jax.ShapeDtypeStruct((B, T, V_COLS), jnp.bfloat16),
            jax.ShapeDtypeStruct((B, T, MIX_WIDTH), jnp.bfloat16),
        ],
        compiler_params=pltpu.CompilerParams(
            dimension_semantics=("arbitrary", "arbitrary"), vmem_limit_bytes=VMEM_LIMIT),
        name=f"proj_l{layer}",
    )(xl, xc, mod, w_in_p, gains, w_uq_p, w_ukv_p, cos_t, sin_t)


def _scores(qm, kt):
    return jnp.dot(qm, kt, preferred_element_type=jnp.float32)


def _probs(s):
    m = jnp.max(s, axis=-1, keepdims=True)
    return jnp.exp2(s - m).astype(jnp.bfloat16)


def _weighted_values(p, vext):
    o = jnp.dot(p, vext, preferred_element_type=jnp.float32)
    return o / o[:, 64:65]


def _softmax_maps():
    maps = []
    for hd in range(GQA_HEADS):
        c, half = hd % 3, hd // 3
        maps.append((Q_OFF_A + c * LANES, (64 * half, 64 * half + 64), KT_OFF_A, V_CH_A + half))
    for hd in range(DIFF_HEADS):
        c = hd // 2
        for j in (2 * (hd % 2), 2 * (hd % 2) + 1):
            maps.append((Q_OFF_D + c * LANES, (32 * j, 32 * j + 32), KT_OFF_D + c * LANES, V_CH_D + hd))
    for hd in range(MLA_HEADS):
        maps.append((Q_OFF_M + hd * LANES, None, KT_OFF_M + hd * LANES, V_CH_M + hd))
    return maps


N_MAPS = GQA_HEADS + 2 * DIFF_HEADS + MLA_HEADS


def _map_scores(i, q_ref, kt_ref):
    q_off, sel, kt_off, _ = _softmax_maps()[i]
    qc = q_ref[:, q_off:q_off + LANES]
    if sel is not None:
        lane = _lane_iota(qc.shape[0])
        qc = jnp.where((lane >= sel[0]) & (lane < sel[1]), qc, jnp.zeros((), qc.dtype))
    return _scores(qc, kt_ref[kt_off:kt_off + LANES, :])


def _map_values(i, p, v_ref):
    ch = _softmax_maps()[i][3]
    return _weighted_values(p, v_ref[:, ch * LANES:(ch + 1) * LANES])


def _gated_mix(outs, g_ref, vecs_ref, dl_ref, lam_init):
    rows = g_ref.shape[0]
    lo = _lane_iota(rows) < 64
    dl = dl_ref[...]
    lam = (jnp.exp(jnp.sum(dl[0:1] * dl[1:2], axis=-1, keepdims=True))
           - jnp.exp(jnp.sum(dl[2:3] * dl[3:4], axis=-1, keepdims=True)) + lam_init)
    subln = vecs_ref[2:3, 0:LANES] * (1.0 - lam_init)

    heads = outs[:GQA_HEADS]
    for hd in range(DIFF_HEADS):
        od = outs[GQA_HEADS + 2 * hd] - lam * outs[GQA_HEADS + 2 * hd + 1]
        ms = jnp.sum(jnp.where(lo, od * od, 0.0), axis=-1, keepdims=True) / DIFF_V_DIM
        heads.append(od * lax.rsqrt(ms + EPS) * subln)
    heads += outs[GQA_HEADS + 2 * DIFF_HEADS:]

    chunks = [jnp.where(lo, heads[2 * k], pltpu.roll(heads[2 * k + 1], 64, 1))
              for k in range(MIX_WIDTH // LANES)]
    hmix = jnp.concatenate(chunks, axis=-1) * g_ref[...].astype(jnp.float32)
    return hmix.astype(jnp.bfloat16)


def _project_residual_norm(hmix, x_ref, mod_ref, wout_ref, vecs_ref, o_ref):
    y = jnp.dot(hmix, wout_ref[...], preferred_element_type=jnp.float32)
    gate = mod_ref[...][:, 2 * D_MODEL:3 * D_MODEL]
    r = DEEPNORM_ALPHA * x_ref[...] + gate * y
    mu = jnp.mean(r, axis=-1, keepdims=True)
    rc = r - mu
    var = jnp.mean(rc * rc, axis=-1, keepdims=True)
    o_ref[...] = rc * lax.rsqrt(var + EPS) * vecs_ref[0:1] + vecs_ref[1:2]


def _attn_ctx_kernel(q_ref, kt_ref, v_ref, g_ref, x_ref, mod_ref, wout_ref, vecs_ref, dl_ref, o_ref, *, lam_init):
    all_p = [_probs(_map_scores(i, q_ref, kt_ref)) for i in range(N_MAPS)]
    outs = [_map_values(i, p, v_ref) for i, p in enumerate(all_p)]
    hmix = _gated_mix(outs, g_ref, vecs_ref, dl_ref, lam_init)
    _project_residual_norm(hmix, x_ref, mod_ref, wout_ref, vecs_ref, o_ref)


SCORES_AHEAD = 3
FINISH_AT = N_MAPS - 2


def _attn_latent_kernel(q_ref, kt_ref, v_ref, g_ref, x_ref, mod_ref, wout_ref, vecs_ref, dl_ref, o_ref,
                        outs_ref, *, lam_init, n_blocks):
    t = pl.program_id(0)

    @pl.when(t == 0)
    def _():
        outs_ref[...] = jnp.zeros(outs_ref.shape, outs_ref.dtype)

    def previous_mix():
        return _gated_mix([outs_ref[i] for i in range(N_MAPS)], g_ref, vecs_ref, dl_ref, lam_init)

    @pl.when(t < n_blocks)
    def _():
        pending = [_map_scores(i, q_ref, kt_ref) for i in range(SCORES_AHEAD)]
        hmix = previous_mix()
        p_cur = _probs(pending.pop(0))
        for i in range(N_MAPS):
            if i + SCORES_AHEAD < N_MAPS:
                pending.append(_map_scores(i + SCORES_AHEAD, q_ref, kt_ref))
            p_next = _probs(pending.pop(0)) if i + 1 < N_MAPS else None
            if i == FINISH_AT:
                _project_residual_norm(hmix, x_ref, mod_ref, wout_ref, vecs_ref, o_ref)
            outs_ref[i] = _map_values(i, p_cur, v_ref)
            p_cur = p_next

    @pl.when(t == n_blocks)
    def _():
        _project_residual_norm(previous_mix(), x_ref, mod_ref, wout_ref, vecs_ref, o_ref)


def _attn_latent_call(layer, q, kt, v, g, xl, mod, w_out_b, vecs, dl, lam_init):
    B, S, _ = xl.shape
    T = kt.shape[2]
    per_batch = S // ATTN_ROWS
    n_blocks = B * per_batch

    def cur(t):
        c = jnp.minimum(t, n_blocks - 1)
        return c // per_batch, c % per_batch

    def prev(t):
        p = jnp.maximum(t - 1, 0)
        return p // per_batch, p % per_batch

    kernel = functools.partial(_attn_latent_kernel, lam_init=lam_init, n_blocks=n_blocks)
    return pl.pallas_call(
        kernel,
        grid=(n_blocks + 1,),
        in_specs=[
            pl.BlockSpec((None, ATTN_ROWS, Q_COLS), lambda t: (*cur(t), 0)),
            pl.BlockSpec((None, KT_ROWS, T), lambda t: (cur(t)[0], 0, 0), pipeline_mode=pl.Buffered(1)),
            pl.BlockSpec((None, T, V_COLS), lambda t: (cur(t)[0], 0, 0), pipeline_mode=pl.Buffered(1)),
            pl.BlockSpec((None, ATTN_ROWS, MIX_WIDTH), lambda t: (*prev(t), 0)),
            pl.BlockSpec((None, ATTN_ROWS, D_MODEL), lambda t: (*prev(t), 0)),
            pl.BlockSpec((None, None, 1, 3 * D_MODEL), lambda t: (layer, prev(t)[0], 0, 0)),
            pl.BlockSpec((None, MIX_WIDTH, D_MODEL), lambda t: (layer, 0, 0), pipeline_mode=pl.Buffered(1)),
            pl.BlockSpec((None, 8, D_MODEL), lambda t: (layer, 0, 0)),
            pl.BlockSpec((None, 4, DIFF_QK_DIM), lambda t: (layer, 0, 0)),
        ],
        out_specs=pl.BlockSpec((None, ATTN_ROWS, D_MODEL), lambda t: (*prev(t), 0)),
        out_shape=jax.ShapeDtypeStruct((B, S, D_MODEL), jnp.float32),
        scratch_shapes=[pltpu.VMEM((N_MAPS, ATTN_ROWS, LANES), jnp.float32)],
        compiler_params=pltpu.CompilerParams(
            dimension_semantics=("arbitrary",), vmem_limit_bytes=VMEM_LIMIT),
        name=f"attn_latent_l{layer}",
    )(q, kt, v, g, xl, mod, w_out_b, vecs, dl)


def _attn_ctx_call(layer, q, kt, v, g, xc, mod, w_out_b, vecs, dl, lam_init):
    B = xc.shape[0]
    blk = (kt.shape[2] - CTX_LEN) // CTX_LEN
    kernel = functools.partial(_attn_ctx_kernel, lam_init=lam_init)
    return pl.pallas_call(
        kernel,
        grid=(B,),
        in_specs=[
            pl.BlockSpec((None, CTX_LEN, Q_COLS), lambda b: (b, blk, 0)),
            pl.BlockSpec((None, KT_ROWS, CTX_LEN), lambda b: (b, 0, blk)),
            pl.BlockSpec((None, CTX_LEN, V_COLS), lambda b: (b, blk, 0)),
            pl.BlockSpec((None, CTX_LEN, MIX_WIDTH), lambda b: (b, blk, 0)),
            pl.BlockSpec((None, CTX_LEN, D_MODEL), lambda b: (b, 0, 0)),
            pl.BlockSpec((None, None, 1, 3 * D_MODEL), lambda b: (layer, CTX_ROW, 0, 0)),
            pl.BlockSpec((None, MIX_WIDTH, D_MODEL), lambda b: (layer, 0, 0)),
            pl.BlockSpec((None, 8, D_MODEL), lambda b: (layer, 0, 0)),
            pl.BlockSpec((None, 4, DIFF_QK_DIM), lambda b: (layer, 0, 0)),
        ],
        out_specs=pl.BlockSpec((None, CTX_LEN, D_MODEL), lambda b: (b, 0, 0)),
        out_shape=jax.ShapeDtypeStruct((B, CTX_LEN, D_MODEL), jnp.float32),
        compiler_params=pltpu.CompilerParams(
            dimension_semantics=("arbitrary",), vmem_limit_bytes=VMEM_LIMIT),
        name=f"attn_ctx_l{layer}",
    )(q, kt, v, g, xc, mod, w_out_b, vecs, dl)


def _rope_tables(seq):
    rows_n = seq // GRID_W
    row = jnp.repeat(jnp.arange(rows_n, dtype=jnp.int32), GRID_W)
    col = jnp.tile(jnp.arange(GRID_W, dtype=jnp.int32), rows_n)

    def tables(pos, dim):
        freqs = ROPE_THETA ** (-jnp.arange(0, dim, 2, dtype=jnp.float32) / dim)
        ang = pos.astype(jnp.float32)[:, None] * freqs[None, :]
        return jnp.cos(ang), jnp.sin(ang)

    def head_tables(head_rot_dim):
        half = head_rot_dim // 2
        cr, sr = tables(row, half)
        cc, sc = tables(col, half)
        return (jnp.concatenate([cr, cr, cc, cc], axis=-1),
                jnp.concatenate([-sr, sr, -sc, sc], axis=-1))

    c64, s64 = head_tables(GQA_HEAD_DIM)
    c32, s32 = head_tables(DIFF_QK_DIM)
    ones, zeros = jnp.ones((seq, 64), jnp.float32), jnp.zeros((seq, 64), jnp.float32)
    cos_l = jnp.stack([jnp.tile(c64, (1, 2)), jnp.tile(c32, (1, 4)),
                       jnp.concatenate([ones, c32, ones[:, :32]], axis=-1)])
    sin_l = jnp.stack([jnp.tile(s64, (1, 2)), jnp.tile(s32, (1, 4)),
                       jnp.concatenate([zeros, s32, zeros[:, :32]], axis=-1)])
    cos_t = jnp.concatenate([cos_l, jnp.ones((3, CTX_LEN, LANES), jnp.float32)], axis=1)
    sin_t = jnp.concatenate([sin_l, jnp.zeros((3, CTX_LEN, LANES), jnp.float32)], axis=1)
    return cos_t, sin_t


def _in_proj_columns():
    sizes = (384, 128, 128, 256, 256, 256, 192, 128, 32, 384, 256, 384)
    starts = np.concatenate([[0], np.cumsum(sizes)[:-1]])
    (s_gq, s_gk, s_gv, s_dq, s_dk, s_dv, s_mq, s_mkv, s_mkr, s_ga, _, _) = [int(s) for s in starts]
    cols = np.full((IN_COLS_P,), -1, np.int64)
    for c in range(3):
        cols[OFF_GQ + c * 128:OFF_GQ + c * 128 + 64] = s_gq + c * 64 + np.arange(64)
        cols[OFF_GQ + c * 128 + 64:OFF_GQ + (c + 1) * 128] = s_gq + (3 + c) * 64 + np.arange(64)
    cols[OFF_GK:OFF_GK + 128] = s_gk + np.arange(128)
    cols[OFF_GV:OFF_GV + 128] = s_gv + np.arange(128)
    cols[OFF_DQ:OFF_DQ + 256] = s_dq + np.arange(256)
    cols[OFF_DK:OFF_DK + 256] = s_dk + np.arange(256)
    cols[OFF_DV:OFF_DV + 256] = s_dv + np.arange(256)
    cols[OFF_GATE:OFF_GATE + 1024] = s_ga + np.arange(1024)
    cols[OFF_MKV:OFF_MKV + 128] = s_mkv + np.arange(128)
    cols[OFF_MQ:OFF_MQ + 192] = s_mq + np.arange(192)
    cols[OFF_MKR:OFF_MKR + 32] = s_mkr + np.arange(32)
    return cols


def _gather_cols(w, cols):
    parts, start = [], 0
    for j in range(1, len(cols) + 1):
        run_ends = (j == len(cols) or (cols[j] < 0) != (cols[start] < 0)
                    or (cols[start] >= 0 and cols[j] != cols[j - 1] + 1))
        if run_ends:
            if cols[start] < 0:
                parts.append(jnp.zeros(w.shape[:-1] + (j - start,), w.dtype))
            else:
                parts.append(w[..., int(cols[start]):int(cols[j - 1]) + 1])
            start = j
    return jnp.concatenate(parts, axis=-1)


def kernel(x, c, ctx, c_ctx, w_ada, b_ada, w_in, gqa_q_g, gqa_k_g, diff_lq1, diff_lk1, diff_lq2, diff_lk2,
           diff_subln_g, mla_q_g, w_uq, mla_kv_g, w_ukv, w_out, ln_g, ln_b):
    B, S, D = x.shape
    f32, bf16 = jnp.float32, jnp.bfloat16

    w_in_p = _gather_cols(w_in, _in_proj_columns()).astype(bf16)
    uq_cols = np.full((MLA_HEADS * LANES,), -1, np.int64)
    ukv_cols = np.full((2 * MLA_HEADS * LANES,), -1, np.int64)
    for hh in range(MLA_HEADS):
        uq_cols[hh * LANES:hh * LANES + 96] = hh * 96 + np.arange(96)
        ukv_cols[hh * LANES:hh * LANES + 64] = hh * 128 + np.arange(64)
        ukv_cols[(MLA_HEADS + hh) * LANES:(MLA_HEADS + hh) * LANES + 64] = hh * 128 + 64 + np.arange(64)
    w_uq_p = jnp.pad(_gather_cols(w_uq, uq_cols), ((0, 0), (0, 2 * LANES - MLA_Q_RANK), (0, 0))).astype(bf16)
    w_ukv_p = _gather_cols(w_ukv, ukv_cols).astype(bf16)
    w_out_b = w_out.astype(bf16)

    def row(vec, width):
        return jnp.pad(vec, ((0, 0), (0, width - vec.shape[-1])))[:, None, :]

    gains = jnp.concatenate([
        row(jnp.tile(gqa_q_g, (1, 2)), 2 * LANES), row(jnp.tile(gqa_k_g, (1, 2)), 2 * LANES),
        row(mla_q_g, 2 * LANES), row(mla_kv_g, 2 * LANES),
        jnp.zeros((DEPTH, 4, 2 * LANES), f32)], axis=1)
    vecs = jnp.concatenate([
        row(ln_g, D), row(ln_b, D), row(jnp.tile(diff_subln_g, (1, 2)), D),
        jnp.zeros((DEPTH, 5, D), f32)], axis=1)
    dl = jnp.stack([diff_lq1, diff_lk1, diff_lq2, diff_lk2], axis=1)
    cos_t, sin_t = _rope_tables(S)

    cc = jnp.concatenate([c, c_ctx[None, :], jnp.zeros((ADA_ROWS - B - 1, D), f32)], axis=0)
    mod = _ada_call(cc, w_ada, b_ada[:, None, :])[:, :, None, :]

    xl, xc = x, ctx
    for layer in range(DEPTH):
        lam_init = 0.8 - 0.6 * math.exp(-0.3 * layer)
        q, kt, v, g = _proj_call(layer, xl, xc, mod, w_in_p, gains, w_uq_p, w_ukv_p, cos_t, sin_t)
        xl_new = _attn_latent_call(layer, q, kt, v, g, xl, mod, w_out_b, vecs, dl, lam_init)
        if layer < DEPTH - 1:
            xc = _attn_ctx_call(layer, q, kt, v, g, xc, mod, w_out_b, vecs, dl, lam_init)
        xl = xl_new
    return xl
```

```python
import functools
import math

import numpy as np
import jax
import jax.numpy as jnp
from jax import lax
from jax.experimental import pallas as pl
from jax.experimental.pallas import tpu as pltpu

D_MODEL = 1024
DEPTH = 4
GRID_W = 64
CTX_LEN = 256
ROPE_THETA = 10000.0
EPS = 1e-6

GQA_HEADS = 6
GQA_KV_HEADS = 2
GQA_HEAD_DIM = 64
DIFF_HEADS = 4
DIFF_QK_DIM = 32
DIFF_V_DIM = 64
MLA_HEADS = 6
MLA_NOPE = 64
MLA_ROPE = 32
MLA_V = 64
MLA_Q_RANK = 192
MLA_KV_RANK = 128
MIX_WIDTH = 1024

DEEPNORM_ALPHA = (2.0 * DEPTH) ** 0.25
LOG2_E = math.log2(math.e)
GQA_QSCALE = GQA_HEAD_DIM ** -0.5 * LOG2_E
DIFF_QSCALE = DIFF_QK_DIM ** -0.5 * LOG2_E
MLA_QSCALE = (MLA_NOPE + MLA_ROPE) ** -0.5 * LOG2_E

LANES = 128
ROW_BLOCK = 256
ATTN_ROWS = 256
ADA_ROWS = 16
CTX_ROW = 8
ADA_COL_BLOCK = 512

OFF_GQ, OFF_GK, OFF_GV = 0, 384, 512
OFF_DQ, OFF_DK, OFF_DV = 640, 896, 1152
OFF_GATE = 1408
OFF_MKV = 2432
OFF_MQ = 2560
OFF_MKR = 2752
IN_COLS_P = 2816

Q_OFF_A, Q_OFF_D, Q_OFF_M = 0, 384, 640
Q_COLS = 1408
KT_OFF_A, KT_OFF_D, KT_OFF_M = 0, 128, 384
KT_ROWS = 1152
V_CH_A, V_CH_D, V_CH_M = 0, 2, 6
V_COLS = 12 * LANES

VMEM_LIMIT = 56 * 1024 * 1024


def _lane_iota(rows, cols=LANES):
    return lax.broadcasted_iota(jnp.int32, (rows, cols), 1)


def _swap_halves(x, half):
    lane = _lane_iota(x.shape[0])
    fwd = pltpu.roll(x, LANES - half, 1)
    bwd = pltpu.roll(x, half, 1)
    return jnp.where((lane % (2 * half)) < half, fwd, bwd)


def _rope(x, cos, sin, half):
    return x * cos + _swap_halves(x, half) * sin


def _ada_kernel(cc_ref, w_ref, b_ref, o_ref):
    cc = cc_ref[...]
    a = (cc * jax.nn.sigmoid(cc)).astype(jnp.bfloat16)
    w = w_ref[...].astype(jnp.bfloat16)
    o_ref[...] = jnp.dot(a, w, preferred_element_type=jnp.float32) + b_ref[...]


def _ada_call(cc, w_ada, b_ada3):
    n_col = (3 * D_MODEL) // ADA_COL_BLOCK
    return pl.pallas_call(
        _ada_kernel,
        grid=(DEPTH, n_col),
        in_specs=[
            pl.BlockSpec((ADA_ROWS, D_MODEL), lambda l, j: (0, 0)),
            pl.BlockSpec((None, D_MODEL, ADA_COL_BLOCK), lambda l, j: (l, 0, j)),
            pl.BlockSpec((None, 1, ADA_COL_BLOCK), lambda l, j: (l, 0, j)),
        ],
        out_specs=pl.BlockSpec((None, ADA_ROWS, ADA_COL_BLOCK), lambda l, j: (l, 0, j)),
        out_shape=jax.ShapeDtypeStruct((DEPTH, ADA_ROWS, 3 * D_MODEL), jnp.float32),
        name="ada_modulation",
    )(cc, w_ada, b_ada3)


def _modulated_in_proj(is_ctx, xl_ref, xc_ref, mod_ref, win_ref, z_ref):
    x = jnp.where(is_ctx, xc_ref[...], xl_ref[...])
    mu = jnp.mean(x, axis=-1, keepdims=True)
    xc = x - mu
    var = jnp.mean(xc * xc, axis=-1, keepdims=True)
    mod = mod_ref[...]
    shift = mod[:, 0:D_MODEL]
    scale = mod[:, D_MODEL:2 * D_MODEL]
    h = (xc * lax.rsqrt(var + EPS) * (1.0 + scale) + shift).astype(jnp.bfloat16)
    z_ref[...] = jnp.dot(h, win_ref[...], preferred_element_type=jnp.float32)


def _proj_kernel(xl_ref, xc_ref, mod_ref, win_ref, gains_ref, wuq_ref, wukv_ref, cos_ref, sin_ref,
                 q_ref, kt_ref, v_ref, g_ref, z_even, z_odd, *, n_blocks, blocks_per_batch):
    t = pl.program_id(0)
    is_ctx = jnp.minimum(t, n_blocks - 1) % blocks_per_batch == blocks_per_batch - 1

    @pl.when(t == 0)
    def _():
        z_odd[...] = jnp.zeros(z_odd.shape, z_odd.dtype)

    @pl.when(t % 2 == 0)
    def _():
        _modulated_in_proj(is_ctx, xl_ref, xc_ref, mod_ref, win_ref, z_even)
        _split_groups(z_odd, gains_ref, wuq_ref, wukv_ref, cos_ref, sin_ref, q_ref, kt_ref, v_ref, g_ref)

    @pl.when(t % 2 == 1)
    def _():
        _modulated_in_proj(is_ctx, xl_ref, xc_ref, mod_ref, win_ref, z_odd)
        _split_groups(z_even, gains_ref, wuq_ref, wukv_ref, cos_ref, sin_ref, q_ref, kt_ref, v_ref, g_ref)


def _split_groups(z, gains_ref, wuq_ref, wukv_ref, cos_ref, sin_ref, q_ref, kt_ref, v_ref, g_ref):
    rows = z.shape[0]
    lane = _lane_iota(rows)
    lo = lane < 64
    one_at_64 = (lane == 64).astype(jnp.float32)

    gains = gains_ref[...]
    g_gq = gains[0:1, 0:LANES]
    g_gk = gains[1:2, 0:LANES]
    g_mq = gains[2:3, :]
    g_mkv = gains[3:4, 0:LANES]
    cos_a, sin_a = cos_ref[0], sin_ref[0]
    cos_d, sin_d = cos_ref[1], sin_ref[1]
    cos_m, sin_m = cos_ref[2], sin_ref[2]

    def head_rms(zc, gain):
        sq = zc * zc
        s_lo = jnp.sum(jnp.where(lo, sq, 0.0), axis=-1, keepdims=True)
        s_hi = jnp.sum(jnp.where(lo, 0.0, sq), axis=-1, keepdims=True)
        r = jnp.where(lo, lax.rsqrt(s_lo / GQA_HEAD_DIM + EPS), lax.rsqrt(s_hi / GQA_HEAD_DIM + EPS))
        return zc * r * gain

    for c in range(3):
        zc = z[:, OFF_GQ + c * LANES:OFF_GQ + (c + 1) * LANES]
        y = _rope(head_rms(zc, g_gq), cos_a, sin_a, 16) * GQA_QSCALE
        q_ref[:, Q_OFF_A + c * LANES:Q_OFF_A + (c + 1) * LANES] = y.astype(q_ref.dtype)
    yk = _rope(head_rms(z[:, OFF_GK:OFF_GK + LANES], g_gk), cos_a, sin_a, 16)
    kt_ref[KT_OFF_A:KT_OFF_A + LANES, :] = yk.T.astype(kt_ref.dtype)

    def store_v_pair(zv, chunk):
        v_lo = jnp.where(lo, zv, one_at_64)
        v_hi = jnp.where(lo, pltpu.roll(zv, 64, 1), one_at_64)
        v_ref[:, chunk * LANES:(chunk + 1) * LANES] = v_lo.astype(v_ref.dtype)
        v_ref[:, (chunk + 1) * LANES:(chunk + 2) * LANES] = v_hi.astype(v_ref.dtype)

    store_v_pair(z[:, OFF_GV:OFF_GV + LANES], V_CH_A)

    for c in range(2):
        zq = z[:, OFF_DQ + c * LANES:OFF_DQ + (c + 1) * LANES]
        yq = _rope(zq, cos_d, sin_d, 8) * DIFF_QSCALE
        q_ref[:, Q_OFF_D + c * LANES:Q_OFF_D + (c + 1) * LANES] = yq.astype(q_ref.dtype)
        zk = z[:, OFF_DK + c * LANES:OFF_DK + (c + 1) * LANES]
        yk = _rope(zk, cos_d, sin_d, 8)
        kt_ref[KT_OFF_D + c * LANES:KT_OFF_D + (c + 1) * LANES, :] = yk.T.astype(kt_ref.dtype)
        store_v_pair(z[:, OFF_DV + c * LANES:OFF_DV + (c + 1) * LANES], V_CH_D + 2 * c)

    zg = z[:, OFF_GATE:OFF_GATE + MIX_WIDTH]
    g_ref[...] = (zg * jax.nn.sigmoid(zg)).astype(g_ref.dtype)

    ckv = z[:, OFF_MKV:OFF_MKV + MLA_KV_RANK]
    ms = jnp.mean(ckv * ckv, axis=-1, keepdims=True)
    cn = (ckv * lax.rsqrt(ms + EPS) * g_mkv).astype(jnp.bfloat16)
    kv = jnp.dot(cn, wukv_ref[...], preferred_element_type=jnp.float32)
    last = z[:, IN_COLS_P - LANES:IN_COLS_P]
    kpe = jnp.where((lane >= 64) & (lane < 96), last, 0.0)
    kpe = _rope(kpe, cos_m, sin_m, 8)
    n_m = MLA_HEADS * LANES
    for hh in range(MLA_HEADS):
        k_h = kv[:, hh * LANES:(hh + 1) * LANES] + kpe
        kt_ref[KT_OFF_M + hh * LANES:KT_OFF_M + (hh + 1) * LANES, :] = k_h.T.astype(kt_ref.dtype)
        v_h = kv[:, n_m + hh * LANES:n_m + (hh + 1) * LANES] + one_at_64
        v_ref[:, (V_CH_M + hh) * LANES:(V_CH_M + hh + 1) * LANES] = v_h.astype(v_ref.dtype)

    zq = z[:, OFF_MQ:OFF_MQ + 2 * LANES]
    lane2 = _lane_iota(rows, 2 * LANES)
    msq = jnp.sum(jnp.where(lane2 < MLA_Q_RANK, zq * zq, 0.0), axis=-1, keepdims=True) / MLA_Q_RANK
    qn = (zq * lax.rsqrt(msq + EPS) * g_mq).astype(jnp.bfloat16)
    qm = jnp.dot(qn, wuq_ref[...], preferred_element_type=jnp.float32)
    for hh in range(MLA_HEADS):
        yq = _rope(qm[:, hh * LANES:(hh + 1) * LANES], cos_m, sin_m, 8) * MLA_QSCALE
        q_ref[:, Q_OFF_M + hh * LANES:Q_OFF_M + (hh + 1) * LANES] = yq.astype(q_ref.dtype)


def _proj_call(layer, xl, xc, mod, w_in_p, gains, w_uq_p, w_ukv_p, cos_t, sin_t):
    B, S, _ = xl.shape
    n_lat = S // ROW_BLOCK
    nblk = n_lat + 1
    T = S + CTX_LEN
    n_blocks = B * nblk

    def cur(t):
        c = jnp.minimum(t, n_blocks - 1)
        return c // nblk, c % nblk

    def prev(t):
        p = jnp.maximum(t - 1, 0)
        return p // nblk, p % nblk

    def mod_map(t):
        b, i = cur(t)
        return (layer, jnp.where(i == n_lat, CTX_ROW, b), 0, 0)

    kernel = functools.partial(_proj_kernel, n_blocks=n_blocks, blocks_per_batch=nblk)
    return pl.pallas_call(
        kernel,
        grid=(n_blocks + 1,),
        in_specs=[
            pl.BlockSpec((None, ROW_BLOCK, D_MODEL), lambda t: (cur(t)[0], jnp.minimum(cur(t)[1], n_lat - 1), 0)),
            pl.BlockSpec((None, CTX_LEN, D_MODEL), lambda t: (cur(t)[0], 0, 0)),
            pl.BlockSpec((None, None, 1, 3 * D_MODEL), mod_map),
            pl.BlockSpec((None, D_MODEL, IN_COLS_P), lambda t: (layer, 0, 0)),
            pl.BlockSpec((None, 8, 2 * LANES), lambda t: (layer, 0, 0)),
            pl.BlockSpec((None, 2 * LANES, MLA_HEADS * LANES), lambda t: (layer, 0, 0)),
            pl.BlockSpec((None, MLA_KV_RANK, 2 * MLA_HEADS * LANES), lambda t: (layer, 0, 0)),
            pl.BlockSpec((3, ROW_BLOCK, LANES), lambda t: (0, prev(t)[1], 0)),
            pl.BlockSpec((3, ROW_BLOCK, LANES), lambda t: (0, prev(t)[1], 0)),
        ],
        out_specs=[
            pl.BlockSpec((None, ROW_BLOCK, Q_COLS), lambda t: (*prev(t), 0)),
            pl.BlockSpec((None, KT_ROWS, ROW_BLOCK), lambda t: (prev(t)[0], 0, prev(t)[1])),
            pl.BlockSpec((None, ROW_BLOCK, V_COLS), lambda t: (*prev(t), 0)),
            pl.BlockSpec((None, ROW_BLOCK, MIX_WIDTH), lambda t: (*prev(t), 0)),
        ],
        out_shape=[
            jax.ShapeDtypeStruct((B, T, Q_COLS), jnp.bfloat16),
            jax.ShapeDtypeStruct((B, KT_ROWS, T), jnp.bfloat16),
            jax.ShapeDtypeStruct((B, T, V_COLS), jnp.bfloat16),
            jax.ShapeDtypeStruct((B, T, MIX_WIDTH), jnp.bfloat16),
        ],
        scratch_shapes=[pltpu.VMEM((ROW_BLOCK, IN_COLS_P), jnp.float32),
                        pltpu.VMEM((ROW_BLOCK, IN_COLS_P), jnp.float32)],
        compiler_params=pltpu.CompilerParams(
            dimension_semantics=("arbitrary",), vmem_limit_bytes=VMEM_LIMIT),
        name=f"proj_l{layer}",
    )(xl, xc, mod, w_in_p, gains, w_uq_p, w_ukv_p, cos_t, sin_t)


def _scores(qm, kt):
    return jnp.dot(qm, kt, preferred_element_type=jnp.float32)


def _probs(s):
    m = jnp.max(s, axis=-1, keepdims=True)
    return jnp.exp2(s - m).astype(jnp.bfloat16)


def _weighted_values(p, vext):
    o = jnp.dot(p, vext, preferred_element_type=jnp.float32)
    return o / o[:, 64:65]


def _softmax_maps():
    maps = []
    for hd in range(GQA_HEADS):
        c, half = hd % 3, hd // 3
        maps.append((Q_OFF_A + c * LANES, (64 * half, 64 * half + 64), KT_OFF_A, V_CH_A + half))
    for hd in range(DIFF_HEADS):
        c = hd // 2
        for j in (2 * (hd % 2), 2 * (hd % 2) + 1):
            maps.append((Q_OFF_D + c * LANES, (32 * j, 32 * j + 32), KT_OFF_D + c * LANES, V_CH_D + hd))
    for hd in range(MLA_HEADS):
        maps.append((Q_OFF_M + hd * LANES, None, KT_OFF_M + hd * LANES, V_CH_M + hd))
    return maps


N_MAPS = GQA_HEADS + 2 * DIFF_HEADS + MLA_HEADS


def _map_scores(i, q_ref, kt_ref):
    q_off, sel, kt_off, _ = _softmax_maps()[i]
    qc = q_ref[:, q_off:q_off + LANES]
    if sel is not None:
        lane = _lane_iota(qc.shape[0])
        qc = jnp.where((lane >= sel[0]) & (lane < sel[1]), qc, jnp.zeros((), qc.dtype))
    return _scores(qc, kt_ref[kt_off:kt_off + LANES, :])


def _map_values(i, p, v_ref):
    ch = _softmax_maps()[i][3]
    return _weighted_values(p, v_ref[:, ch * LANES:(ch + 1) * LANES])


def _gated_mix(outs, g_ref, vecs_ref, dl_ref, lam_init):
    rows = g_ref.shape[0]
    lo = _lane_iota(rows) < 64
    dl = dl_ref[...]
    lam = (jnp.exp(jnp.sum(dl[0:1] * dl[1:2], axis=-1, keepdims=True))
           - jnp.exp(jnp.sum(dl[2:3] * dl[3:4], axis=-1, keepdims=True)) + lam_init)
    subln = vecs_ref[2:3, 0:LANES] * (1.0 - lam_init)

    heads = outs[:GQA_HEADS]
    for hd in range(DIFF_HEADS):
        od = outs[GQA_HEADS + 2 * hd] - lam * outs[GQA_HEADS + 2 * hd + 1]
        ms = jnp.sum(jnp.where(lo, od * od, 0.0), axis=-1, keepdims=True) / DIFF_V_DIM
        heads.append(od * lax.rsqrt(ms + EPS) * subln)
    heads += outs[GQA_HEADS + 2 * DIFF_HEADS:]

    chunks = [jnp.where(lo, heads[2 * k], pltpu.roll(heads[2 * k + 1], 64, 1))
              for k in range(MIX_WIDTH // LANES)]
    hmix = jnp.concatenate(chunks, axis=-1) * g_ref[...].astype(jnp.float32)
    return hmix.astype(jnp.bfloat16)


def _project_residual_norm(hmix, x_ref, mod_ref, wout_ref, vecs_ref, o_ref):
    y = jnp.dot(hmix, wout_ref[...], preferred_element_type=jnp.float32)
    gate = mod_ref[...][:, 2 * D_MODEL:3 * D_MODEL]
    r = DEEPNORM_ALPHA * x_ref[...] + gate * y
    mu = jnp.mean(r, axis=-1, keepdims=True)
    rc = r - mu
    var = jnp.mean(rc * rc, axis=-1, keepdims=True)
    o_ref[...] = rc * lax.rsqrt(var + EPS) * vecs_ref[0:1] + vecs_ref[1:2]


def _attn_ctx_kernel(q_ref, kt_ref, v_ref, g_ref, x_ref, mod_ref, wout_ref, vecs_ref, dl_ref, o_ref, *, lam_init):
    all_p = [_probs(_map_scores(i, q_ref, kt_ref)) for i in range(N_MAPS)]
    outs = [_map_values(i, p, v_ref) for i, p in enumerate(all_p)]
    hmix = _gated_mix(outs, g_ref, vecs_ref, dl_ref, lam_init)
    _project_residual_norm(hmix, x_ref, mod_ref, wout_ref, vecs_ref, o_ref)


SCORES_AHEAD = 3
FINISH_AT = N_MAPS - 2


def _attn_latent_kernel(q_ref, kt_ref, v_ref, g_ref, x_ref, mod_ref, wout_ref, vecs_ref, dl_ref, o_ref,
                        outs_ref, *, lam_init, n_blocks):
    t = pl.program_id(0)

    @pl.when(t == 0)
    def _():
        outs_ref[...] = jnp.zeros(outs_ref.shape, outs_ref.dtype)

    def previous_mix():
        return _gated_mix([outs_ref[i] for i in range(N_MAPS)], g_ref, vecs_ref, dl_ref, lam_init)

    @pl.when(t < n_blocks)
    def _():
        pending = [_map_scores(i, q_ref, kt_ref) for i in range(SCORES_AHEAD)]
        hmix = previous_mix()
        p_cur = _probs(pending.pop(0))
        for i in range(N_MAPS):
            if i + SCORES_AHEAD < N_MAPS:
                pending.append(_map_scores(i + SCORES_AHEAD, q_ref, kt_ref))
            p_next = _probs(pending.pop(0)) if i + 1 < N_MAPS else None
            if i == FINISH_AT:
                _project_residual_norm(hmix, x_ref, mod_ref, wout_ref, vecs_ref, o_ref)
            outs_ref[i] = _map_values(i, p_cur, v_ref)
            p_cur = p_next

    @pl.when(t == n_blocks)
    def _():
        _project_residual_norm(previous_mix(), x_ref, mod_ref, wout_ref, vecs_ref, o_ref)


def _attn_latent_call(layer, q, kt, v, g, xl, mod, w_out_b, vecs, dl, lam_init):
    B, S, _ = xl.shape
    T = kt.shape[2]
    per_batch = S // ATTN_ROWS
    n_blocks = B * per_batch

    def cur(t):
        c = jnp.minimum(t, n_blocks - 1)
        return c // per_batch, c % per_batch

    def prev(t):
        p = jnp.maximum(t - 1, 0)
        return p // per_batch, p % per_batch

    kernel = functools.partial(_attn_latent_kernel, lam_init=lam_init, n_blocks=n_blocks)
    return pl.pallas_call(
        kernel,
        grid=(n_blocks + 1,),
        in_specs=[
            pl.BlockSpec((None, ATTN_ROWS, Q_COLS), lambda t: (*cur(t), 0)),
            pl.BlockSpec((None, KT_ROWS, T), lambda t: (cur(t)[0], 0, 0)),
            pl.BlockSpec((None, T, V_COLS), lambda t: (cur(t)[0], 0, 0)),
            pl.BlockSpec((None, ATTN_ROWS, MIX_WIDTH), lambda t: (*prev(t), 0)),
            pl.BlockSpec((None, ATTN_ROWS, D_MODEL), lambda t: (*prev(t), 0)),
            pl.BlockSpec((None, None, 1, 3 * D_MODEL), lambda t: (layer, prev(t)[0], 0, 0)),
            pl.BlockSpec((None, MIX_WIDTH, D_MODEL), lambda t: (layer, 0, 0), pipeline_mode=pl.Buffered(1)),
            pl.BlockSpec((None, 8, D_MODEL), lambda t: (layer, 0, 0)),
            pl.BlockSpec((None, 4, DIFF_QK_DIM), lambda t: (layer, 0, 0)),
        ],
        out_specs=pl.BlockSpec((None, ATTN_ROWS, D_MODEL), lambda t: (*prev(t), 0)),
        out_shape=jax.ShapeDtypeStruct((B, S, D_MODEL), jnp.float32),
        scratch_shapes=[pltpu.VMEM((N_MAPS, ATTN_ROWS, LANES), jnp.float32)],
        compiler_params=pltpu.CompilerParams(
            dimension_semantics=("arbitrary",), vmem_limit_bytes=VMEM_LIMIT),
        name=f"attn_latent_l{layer}",
    )(q, kt, v, g, xl, mod, w_out_b, vecs, dl)


def _attn_ctx_call(layer, q, kt, v, g, xc, mod, w_out_b, vecs, dl, lam_init):
    B = xc.shape[0]
    blk = (kt.shape[2] - CTX_LEN) // CTX_LEN
    kernel = functools.partial(_attn_ctx_kernel, lam_init=lam_init)
    return pl.pallas_call(
        kernel,
        grid=(B,),
        in_specs=[
            pl.BlockSpec((None, CTX_LEN, Q_COLS), lambda b: (b, blk, 0)),
            pl.BlockSpec((None, KT_ROWS, CTX_LEN), lambda b: (b, 0, blk)),
            pl.BlockSpec((None, CTX_LEN, V_COLS), lambda b: (b, blk, 0)),
            pl.BlockSpec((None, CTX_LEN, MIX_WIDTH), lambda b: (b, blk, 0)),
            pl.BlockSpec((None, CTX_LEN, D_MODEL), lambda b: (b, 0, 0)),
            pl.BlockSpec((None, None, 1, 3 * D_MODEL), lambda b: (layer, CTX_ROW, 0, 0)),
            pl.BlockSpec((None, MIX_WIDTH, D_MODEL), lambda b: (layer, 0, 0)),
            pl.BlockSpec((None, 8, D_MODEL), lambda b: (layer, 0, 0)),
            pl.BlockSpec((None, 4, DIFF_QK_DIM), lambda b: (layer, 0, 0)),
        ],
        out_specs=pl.BlockSpec((None, CTX_LEN, D_MODEL), lambda b: (b, 0, 0)),
        out_shape=jax.ShapeDtypeStruct((B, CTX_LEN, D_MODEL), jnp.float32),
        compiler_params=pltpu.CompilerParams(
            dimension_semantics=("arbitrary",), vmem_limit_bytes=VMEM_LIMIT),
        name=f"attn_ctx_l{layer}",
    )(q, kt, v, g, xc, mod, w_out_b, vecs, dl)


def _rope_tables(seq):
    rows_n = seq // GRID_W
    row = jnp.repeat(jnp.arange(rows_n, dtype=jnp.int32), GRID_W)
    col = jnp.tile(jnp.arange(GRID_W, dtype=jnp.int32), rows_n)

    def tables(pos, dim):
        freqs = ROPE_THETA ** (-jnp.arange(0, dim, 2, dtype=jnp.float32) / dim)
        ang = pos.astype(jnp.float32)[:, None] * freqs[None, :]
        return jnp.cos(ang), jnp.sin(ang)

    def head_tables(head_rot_dim):
        half = head_rot_dim // 2
        cr, sr = tables(row, half)
        cc, sc = tables(col, half)
        return (jnp.concatenate([cr, cr, cc, cc], axis=-1),
                jnp.concatenate([-sr, sr, -sc, sc], axis=-1))

    c64, s64 = head_tables(GQA_HEAD_DIM)
    c32, s32 = head_tables(DIFF_QK_DIM)
    ones, zeros = jnp.ones((seq, 64), jnp.float32), jnp.zeros((seq, 64), jnp.float32)
    cos_l = jnp.stack([jnp.tile(c64, (1, 2)), jnp.tile(c32, (1, 4)),
                       jnp.concatenate([ones, c32, ones[:, :32]], axis=-1)])
    sin_l = jnp.stack([jnp.tile(s64, (1, 2)), jnp.tile(s32, (1, 4)),
                       jnp.concatenate([zeros, s32, zeros[:, :32]], axis=-1)])
    cos_t = jnp.concatenate([cos_l, jnp.ones((3, CTX_LEN, LANES), jnp.float32)], axis=1)
    sin_t = jnp.concatenate([sin_l, jnp.zeros((3, CTX_LEN, LANES), jnp.float32)], axis=1)
    return cos_t, sin_t


def _in_proj_columns():
    sizes = (384, 128, 128, 256, 256, 256, 192, 128, 32, 384, 256, 384)
    starts = np.concatenate([[0], np.cumsum(sizes)[:-1]])
    (s_gq, s_gk, s_gv, s_dq, s_dk, s_dv, s_mq, s_mkv, s_mkr, s_ga, _, _) = [int(s) for s in starts]
    cols = np.full((IN_COLS_P,), -1, np.int64)
    for c in range(3):
        cols[OFF_GQ + c * 128:OFF_GQ + c * 128 + 64] = s_gq + c * 64 + np.arange(64)
        cols[OFF_GQ + c * 128 + 64:OFF_GQ + (c + 1) * 128] = s_gq + (3 + c) * 64 + np.arange(64)
    cols[OFF_GK:OFF_GK + 128] = s_gk + np.arange(128)
    cols[OFF_GV:OFF_GV + 128] = s_gv + np.arange(128)
    cols[OFF_DQ:OFF_DQ + 256] = s_dq + np.arange(256)
    cols[OFF_DK:OFF_DK + 256] = s_dk + np.arange(256)
    cols[OFF_DV:OFF_DV + 256] = s_dv + np.arange(256)
    cols[OFF_GATE:OFF_GATE + 1024] = s_ga + np.arange(1024)
    cols[OFF_MKV:OFF_MKV + 128] = s_mkv + np.arange(128)
    cols[OFF_MQ:OFF_MQ + 192] = s_mq + np.arange(192)
    cols[OFF_MKR:OFF_MKR + 32] = s_mkr + np.arange(32)
    return cols


def _gather_cols(w, cols):
    parts, start = [], 0
    for j in range(1, len(cols) + 1):
        run_ends = (j == len(cols) or (cols[j] < 0) != (cols[start] < 0)
                    or (cols[start] >= 0 and cols[j] != cols[j - 1] + 1))
        if run_ends:
            if cols[start] < 0:
                parts.append(jnp.zeros(w.shape[:-1] + (j - start,), w.dtype))
            else:
                parts.append(w[..., int(cols[start]):int(cols[j - 1]) + 1])
            start = j
    return jnp.concatenate(parts, axis=-1)


def kernel(x, c, ctx, c_ctx, w_ada, b_ada, w_in, gqa_q_g, gqa_k_g, diff_lq1, diff_lk1, diff_lq2, diff_lk2,
           diff_subln_g, mla_q_g, w_uq, mla_kv_g, w_ukv, w_out, ln_g, ln_b):
    B, S, D = x.shape
    f32, bf16 = jnp.float32, jnp.bfloat16

    w_in_p = _gather_cols(w_in, _in_proj_columns()).astype(bf16)
    uq_cols = np.full((MLA_HEADS * LANES,), -1, np.int64)
    ukv_cols = np.full((2 * MLA_HEADS * LANES,), -1, np.int64)
    for hh in range(MLA_HEADS):
        uq_cols[hh * LANES:hh * LANES + 96] = hh * 96 + np.arange(96)
        ukv_cols[hh * LANES:hh * LANES + 64] = hh * 128 + np.arange(64)
        ukv_cols[(MLA_HEADS + hh) * LANES:(MLA_HEADS + hh) * LANES + 64] = hh * 128 + 64 + np.arange(64)
    w_uq_p = jnp.pad(_gather_cols(w_uq, uq_cols), ((0, 0), (0, 2 * LANES - MLA_Q_RANK), (0, 0))).astype(bf16)
    w_ukv_p = _gather_cols(w_ukv, ukv_cols).astype(bf16)
    w_out_b = w_out.astype(bf16)

    def row(vec, width):
        return jnp.pad(vec, ((0, 0), (0, width - vec.shape[-1])))[:, None, :]

    gains = jnp.concatenate([
        row(jnp.tile(gqa_q_g, (1, 2)), 2 * LANES), row(jnp.tile(gqa_k_g, (1, 2)), 2 * LANES),
        row(mla_q_g, 2 * LANES), row(mla_kv_g, 2 * LANES),
        jnp.zeros((DEPTH, 4, 2 * LANES), f32)], axis=1)
    vecs = jnp.concatenate([
        row(ln_g, D), row(ln_b, D), row(jnp.tile(diff_subln_g, (1, 2)), D),
        jnp.zeros((DEPTH, 5, D), f32)], axis=1)
    dl = jnp.stack([diff_lq1, diff_lk1, diff_lq2, diff_lk2], axis=1)
    cos_t, sin_t = _rope_tables(S)

    cc = jnp.concatenate([c, c_ctx[None, :], jnp.zeros((ADA_ROWS - B - 1, D), f32)], axis=0)
    mod = _ada_call(cc, w_ada, b_ada[:, None, :])[:, :, None, :]

    xl, xc = x, ctx
    for layer in range(DEPTH):
        lam_init = 0.8 - 0.6 * math.exp(-0.3 * layer)
        q, kt, v, g = _proj_call(layer, xl, xc, mod, w_in_p, gains, w_uq_p, w_ukv_p, cos_t, sin_t)
        xl_new = _attn_latent_call(layer, q, kt, v, g, xl, mod, w_out_b, vecs, dl, lam_init)
        if layer < DEPTH - 1:
            xc = _attn_ctx_call(layer, q, kt, v, g, xc, mod, w_out_b, vecs, dl, lam_init)
        xl = xl_new
    return xl
```

```python
import functools
import math

import numpy as np
import jax
import jax.numpy as jnp
from jax import lax
from jax.experimental import pallas as pl
from jax.experimental.pallas import tpu as pltpu

D_MODEL = 1024
DEPTH = 4
GRID_W = 64
CTX_LEN = 256
ROPE_THETA = 10000.0
EPS = 1e-6

GQA_HEADS = 6
GQA_KV_HEADS = 2
GQA_HEAD_DIM = 64
DIFF_HEADS = 4
DIFF_QK_DIM = 32
DIFF_V_DIM = 64
MLA_HEADS = 6
MLA_NOPE = 64
MLA_ROPE = 32
MLA_V = 64
MLA_Q_RANK = 192
MLA_KV_RANK = 128
MIX_WIDTH = 1024

DEEPNORM_ALPHA = (2.0 * DEPTH) ** 0.25
LOG2_E = math.log2(math.e)
GQA_QSCALE = GQA_HEAD_DIM ** -0.5 * LOG2_E
DIFF_QSCALE = DIFF_QK_DIM ** -0.5 * LOG2_E
MLA_QSCALE = (MLA_NOPE + MLA_ROPE) ** -0.5 * LOG2_E

LANES = 128
ROW_BLOCK = 256
ATTN_ROWS = 256
ADA_ROWS = 16
CTX_ROW = 8
ADA_COL_BLOCK = 512

OFF_GQ, OFF_GK, OFF_GV = 0, 384, 512
OFF_DQ, OFF_DK, OFF_DV = 640, 896, 1152
OFF_GATE = 1408
OFF_MKV = 2432
OFF_MQ = 2560
OFF_MKR = 2752
IN_COLS_P = 2816

Q_OFF_A, Q_OFF_D, Q_OFF_M = 0, 384, 640
Q_COLS = 1408
KT_OFF_A, KT_OFF_D, KT_OFF_M = 0, 128, 384
KT_ROWS = 1152
V_CH_A, V_CH_D, V_CH_M = 0, 2, 6
V_COLS = 12 * LANES

VMEM_LIMIT = 56 * 1024 * 1024


def _lane_iota(rows, cols=LANES):
    return lax.broadcasted_iota(jnp.int32, (rows, cols), 1)


def _rope(x, cos, sin):
    return x * cos + pltpu.roll(x, LANES // 2, 1) * sin


def _ada_kernel(cc_ref, w_ref, b_ref, o_ref):
    cc = cc_ref[...]
    a = (cc * jax.nn.sigmoid(cc)).astype(jnp.bfloat16)
    w = w_ref[...].astype(jnp.bfloat16)
    o_ref[...] = jnp.dot(a, w, preferred_element_type=jnp.float32) + b_ref[...]


def _ada_call(cc, w_ada, b_ada3):
    n_col = (3 * D_MODEL) // ADA_COL_BLOCK
    return pl.pallas_call(
        _ada_kernel,
        grid=(DEPTH, n_col),
        in_specs=[
            pl.BlockSpec((ADA_ROWS, D_MODEL), lambda l, j: (0, 0)),
            pl.BlockSpec((None, D_MODEL, ADA_COL_BLOCK), lambda l, j: (l, 0, j)),
            pl.BlockSpec((None, 1, ADA_COL_BLOCK), lambda l, j: (l, 0, j)),
        ],
        out_specs=pl.BlockSpec((None, ADA_ROWS, ADA_COL_BLOCK), lambda l, j: (l, 0, j)),
        out_shape=jax.ShapeDtypeStruct((DEPTH, ADA_ROWS, 3 * D_MODEL), jnp.float32),
        name="ada_modulation",
    )(cc, w_ada, b_ada3)


def _modulated_in_proj(is_ctx, xl_ref, xc_ref, mod_ref, win_ref, z_ref):
    x = jnp.where(is_ctx, xc_ref[...], xl_ref[...])
    mu = jnp.mean(x, axis=-1, keepdims=True)
    xc = x - mu
    var = jnp.mean(xc * xc, axis=-1, keepdims=True)
    mod = mod_ref[...]
    shift = mod[:, 0:D_MODEL]
    scale = mod[:, D_MODEL:2 * D_MODEL]
    h = (xc * lax.rsqrt(var + EPS) * (1.0 + scale) + shift).astype(jnp.bfloat16)
    z_ref[...] = jnp.dot(h, win_ref[...], preferred_element_type=jnp.float32)


def _proj_kernel(xl_ref, xc_ref, mod_ref, win_ref, gains_ref, wuq_ref, wukv_ref, cos_ref, sin_ref,
                 q_ref, kt_ref, v_ref, g_ref, z_even, z_odd, *, n_blocks, blocks_per_batch):
    t = pl.program_id(0)
    is_ctx = jnp.minimum(t, n_blocks - 1) % blocks_per_batch == blocks_per_batch - 1

    @pl.when(t == 0)
    def _():
        z_odd[...] = jnp.zeros(z_odd.shape, z_odd.dtype)

    @pl.when(t % 2 == 0)
    def _():
        _modulated_in_proj(is_ctx, xl_ref, xc_ref, mod_ref, win_ref, z_even)
        _split_groups(z_odd, gains_ref, wuq_ref, wukv_ref, cos_ref, sin_ref, q_ref, kt_ref, v_ref, g_ref)

    @pl.when(t % 2 == 1)
    def _():
        _modulated_in_proj(is_ctx, xl_ref, xc_ref, mod_ref, win_ref, z_odd)
        _split_groups(z_even, gains_ref, wuq_ref, wukv_ref, cos_ref, sin_ref, q_ref, kt_ref, v_ref, g_ref)


def _split_groups(z, gains_ref, wuq_ref, wukv_ref, cos_ref, sin_ref, q_ref, kt_ref, v_ref, g_ref):
    rows = z.shape[0]
    lane = _lane_iota(rows)
    lo = lane < 64
    one_at_64 = (lane == 64).astype(jnp.float32)

    gains = gains_ref[...]
    g_gq = gains[0:1, 0:LANES]
    g_gk = gains[1:2, 0:LANES]
    g_mq = gains[2:3, :]
    g_mkv = gains[3:4, 0:LANES]
    cos_a, sin_a = cos_ref[0], sin_ref[0]
    cos_d, sin_d = cos_ref[1], sin_ref[1]
    cos_m, sin_m = cos_ref[2], sin_ref[2]

    first_head = (lane % 64) < 32

    def head_rms(zc, gain):
        sq = zc * zc
        s_a = jnp.sum(jnp.where(first_head, sq, 0.0), axis=-1, keepdims=True)
        s_b = jnp.sum(jnp.where(first_head, 0.0, sq), axis=-1, keepdims=True)
        r = jnp.where(first_head, lax.rsqrt(s_a / GQA_HEAD_DIM + EPS), lax.rsqrt(s_b / GQA_HEAD_DIM + EPS))
        return zc * r * gain

    for c in range(3):
        zc = z[:, OFF_GQ + c * LANES:OFF_GQ + (c + 1) * LANES]
        y = _rope(head_rms(zc, g_gq), cos_a, sin_a) * GQA_QSCALE
        q_ref[:, Q_OFF_A + c * LANES:Q_OFF_A + (c + 1) * LANES] = y.astype(q_ref.dtype)
    yk = _rope(head_rms(z[:, OFF_GK:OFF_GK + LANES], g_gk), cos_a, sin_a)
    kt_ref[KT_OFF_A:KT_OFF_A + LANES, :] = yk.T.astype(kt_ref.dtype)

    def store_v_pair(zv, chunk):
        v_lo = jnp.where(lo, zv, one_at_64)
        v_hi = jnp.where(lo, pltpu.roll(zv, 64, 1), one_at_64)
        v_ref[:, chunk * LANES:(chunk + 1) * LANES] = v_lo.astype(v_ref.dtype)
        v_ref[:, (chunk + 1) * LANES:(chunk + 2) * LANES] = v_hi.astype(v_ref.dtype)

    store_v_pair(z[:, OFF_GV:OFF_GV + LANES], V_CH_A)

    for c in range(2):
        zq = z[:, OFF_DQ + c * LANES:OFF_DQ + (c + 1) * LANES]
        yq = _rope(zq, cos_d, sin_d) * DIFF_QSCALE
        q_ref[:, Q_OFF_D + c * LANES:Q_OFF_D + (c + 1) * LANES] = yq.astype(q_ref.dtype)
        zk = z[:, OFF_DK + c * LANES:OFF_DK + (c + 1) * LANES]
        yk = _rope(zk, cos_d, sin_d)
        kt_ref[KT_OFF_D + c * LANES:KT_OFF_D + (c + 1) * LANES, :] = yk.T.astype(kt_ref.dtype)
        store_v_pair(z[:, OFF_DV + c * LANES:OFF_DV + (c + 1) * LANES], V_CH_D + 2 * c)

    zg = z[:, OFF_GATE:OFF_GATE + MIX_WIDTH]
    g_ref[...] = (zg * jax.nn.sigmoid(zg)).astype(g_ref.dtype)

    ckv = z[:, OFF_MKV:OFF_MKV + MLA_KV_RANK]
    ms = jnp.mean(ckv * ckv, axis=-1, keepdims=True)
    cn = (ckv * lax.rsqrt(ms + EPS) * g_mkv).astype(jnp.bfloat16)
    kv = jnp.dot(cn, wukv_ref[...], preferred_element_type=jnp.float32)
    last = z[:, IN_COLS_P - LANES:IN_COLS_P]
    kpe = jnp.where(lane < 16, pltpu.roll(last, 64, 1),
                    jnp.where((lane >= 64) & (lane < 80), pltpu.roll(last, LANES - 16, 1), 0.0))
    kpe = _rope(kpe, cos_m, sin_m)
    n_m = MLA_HEADS * LANES
    for hh in range(MLA_HEADS):
        k_h = kv[:, hh * LANES:(hh + 1) * LANES] + kpe
        kt_ref[KT_OFF_M + hh * LANES:KT_OFF_M + (hh + 1) * LANES, :] = k_h.T.astype(kt_ref.dtype)
        v_h = kv[:, n_m + hh * LANES:n_m + (hh + 1) * LANES] + one_at_64
        v_ref[:, (V_CH_M + hh) * LANES:(V_CH_M + hh + 1) * LANES] = v_h.astype(v_ref.dtype)

    zq = z[:, OFF_MQ:OFF_MQ + 2 * LANES]
    lane2 = _lane_iota(rows, 2 * LANES)
    msq = jnp.sum(jnp.where(lane2 < MLA_Q_RANK, zq * zq, 0.0), axis=-1, keepdims=True) / MLA_Q_RANK
    qn = (zq * lax.rsqrt(msq + EPS) * g_mq).astype(jnp.bfloat16)
    qm = jnp.dot(qn, wuq_ref[...], preferred_element_type=jnp.float32)
    for hh in range(MLA_HEADS):
        yq = _rope(qm[:, hh * LANES:(hh + 1) * LANES], cos_m, sin_m) * MLA_QSCALE
        q_ref[:, Q_OFF_M + hh * LANES:Q_OFF_M + (hh + 1) * LANES] = yq.astype(q_ref.dtype)


def _proj_call(layer, xl, xc, mod, w_in_p, gains, w_uq_p, w_ukv_p, cos_t, sin_t):
    B, S, _ = xl.shape
    n_lat = S // ROW_BLOCK
    nblk = n_lat + 1
    T = S + CTX_LEN
    n_blocks = B * nblk

    def cur(t):
        c = jnp.minimum(t, n_blocks - 1)
        return c // nblk, c % nblk

    def prev(t):
        p = jnp.maximum(t - 1, 0)
        return p // nblk, p % nblk

    def mod_map(t):
        b, i = cur(t)
        return (layer, jnp.where(i == n_lat, CTX_ROW, b), 0, 0)

    kernel = functools.partial(_proj_kernel, n_blocks=n_blocks, blocks_per_batch=nblk)
    return pl.pallas_call(
        kernel,
        grid=(n_blocks + 1,),
        in_specs=[
            pl.BlockSpec((None, ROW_BLOCK, D_MODEL), lambda t: (cur(t)[0], jnp.minimum(cur(t)[1], n_lat - 1), 0)),
            pl.BlockSpec((None, CTX_LEN, D_MODEL), lambda t: (cur(t)[0], 0, 0)),
            pl.BlockSpec((None, None, 1, 3 * D_MODEL), mod_map),
            pl.BlockSpec((None, D_MODEL, IN_COLS_P), lambda t: (layer, 0, 0)),
            pl.BlockSpec((None, 8, 2 * LANES), lambda t: (layer, 0, 0)),
            pl.BlockSpec((None, 2 * LANES, MLA_HEADS * LANES), lambda t: (layer, 0, 0)),
            pl.BlockSpec((None, MLA_KV_RANK, 2 * MLA_HEADS * LANES), lambda t: (layer, 0, 0)),
            pl.BlockSpec((3, ROW_BLOCK, LANES), lambda t: (0, prev(t)[1], 0)),
            pl.BlockSpec((3, ROW_BLOCK, LANES), lambda t: (0, prev(t)[1], 0)),
        ],
        out_specs=[
            pl.BlockSpec((None, ROW_BLOCK, Q_COLS), lambda t: (*prev(t), 0)),
            pl.BlockSpec((None, KT_ROWS, ROW_BLOCK), lambda t: (prev(t)[0], 0, prev(t)[1])),
            pl.BlockSpec((None, ROW_BLOCK, V_COLS), lambda t: (*prev(t), 0)),
            pl.BlockSpec((None, ROW_BLOCK, MIX_WIDTH), lambda t: (*prev(t), 0)),
        ],
        out_shape=[
            jax.ShapeDtypeStruct((B, T, Q_COLS), jnp.bfloat16),
            jax.ShapeDtypeStruct((B, KT_ROWS, T), jnp.bfloat16),
            jax.ShapeDtypeStruct((B, T, V_COLS), jnp.bfloat16),
            jax.ShapeDtypeStruct((B, T, MIX_WIDTH), jnp.bfloat16),
        ],
        scratch_shapes=[pltpu.VMEM((ROW_BLOCK, IN_COLS_P), jnp.float32),
                        pltpu.VMEM((ROW_BLOCK, IN_COLS_P), jnp.float32)],
        compiler_params=pltpu.CompilerParams(
            dimension_semantics=("arbitrary",), vmem_limit_bytes=VMEM_LIMIT),
        name=f"proj_l{layer}",
    )(xl, xc, mod, w_in_p, gains, w_uq_p, w_ukv_p, cos_t, sin_t)


def _scores(qm, kt):
    return jnp.dot(qm, kt, preferred_element_type=jnp.float32)


def _probs(s):
    m = jnp.max(s, axis=-1, keepdims=True)
    return jnp.exp2(s - m).astype(jnp.bfloat16)


def _weighted_values(p, vext):
    o = jnp.dot(p, vext, preferred_element_type=jnp.float32)
    return o / o[:, 64:65]


def _softmax_maps():
    maps = []
    for hd in range(GQA_HEADS):
        c, half = hd % 3, hd // 3
        maps.append((Q_OFF_A + c * LANES, (32 * half, 32 * half + 32), KT_OFF_A, V_CH_A + half))
    for hd in range(DIFF_HEADS):
        c = hd // 2
        for j in (2 * (hd % 2), 2 * (hd % 2) + 1):
            maps.append((Q_OFF_D + c * LANES, (16 * j, 16 * j + 16), KT_OFF_D + c * LANES, V_CH_D + hd))
    for hd in range(MLA_HEADS):
        maps.append((Q_OFF_M + hd * LANES, None, KT_OFF_M + hd * LANES, V_CH_M + hd))
    return maps


N_MAPS = GQA_HEADS + 2 * DIFF_HEADS + MLA_HEADS


def _map_scores(i, q_ref, kt_ref):
    q_off, sel, kt_off, _ = _softmax_maps()[i]
    qc = q_ref[:, q_off:q_off + LANES]
    if sel is not None:
        lane = _lane_iota(qc.shape[0]) % 64
        qc = jnp.where((lane >= sel[0]) & (lane < sel[1]), qc, jnp.zeros((), qc.dtype))
    return _scores(qc, kt_ref[kt_off:kt_off + LANES, :])


def _map_values(i, p, v_ref):
    ch = _softmax_maps()[i][3]
    return _weighted_values(p, v_ref[:, ch * LANES:(ch + 1) * LANES])


def _gated_mix(outs, g_ref, vecs_ref, dl_ref, lam_init):
    rows = g_ref.shape[0]
    lo = _lane_iota(rows) < 64
    dl = dl_ref[...]
    lam = (jnp.exp(jnp.sum(dl[0:1] * dl[1:2], axis=-1, keepdims=True))
           - jnp.exp(jnp.sum(dl[2:3] * dl[3:4], axis=-1, keepdims=True)) + lam_init)
    subln = vecs_ref[2:3, 0:LANES] * (1.0 - lam_init)

    heads = outs[:GQA_HEADS]
    for hd in range(DIFF_HEADS):
        od = outs[GQA_HEADS + 2 * hd] - lam * outs[GQA_HEADS + 2 * hd + 1]
        ms = jnp.sum(jnp.where(lo, od * od, 0.0), axis=-1, keepdims=True) / DIFF_V_DIM
        heads.append(od * lax.rsqrt(ms + EPS) * subln)
    heads += outs[GQA_HEADS + 2 * DIFF_HEADS:]

    chunks = [jnp.where(lo, heads[2 * k], pltpu.roll(heads[2 * k + 1], 64, 1))
              for k in range(MIX_WIDTH // LANES)]
    hmix = jnp.concatenate(chunks, axis=-1) * g_ref[...].astype(jnp.float32)
    return hmix.astype(jnp.bfloat16)


def _project_residual_norm(hmix, x_ref, mod_ref, wout_ref, vecs_ref, o_ref):
    y = jnp.dot(hmix, wout_ref[...], preferred_element_type=jnp.float32)
    gate = mod_ref[...][:, 2 * D_MODEL:3 * D_MODEL]
    r = DEEPNORM_ALPHA * x_ref[...] + gate * y
    mu = jnp.mean(r, axis=-1, keepdims=True)
    rc = r - mu
    var = jnp.mean(rc * rc, axis=-1, keepdims=True)
    o_ref[...] = rc * lax.rsqrt(var + EPS) * vecs_ref[0:1] + vecs_ref[1:2]


def _attn_ctx_kernel(q_ref, kt_ref, v_ref, g_ref, x_ref, mod_ref, wout_ref, vecs_ref, dl_ref, o_ref, *, lam_init):
    all_p = [_probs(_map_scores(i, q_ref, kt_ref)) for i in range(N_MAPS)]
    outs = [_map_values(i, p, v_ref) for i, p in enumerate(all_p)]
    hmix = _gated_mix(outs, g_ref, vecs_ref, dl_ref, lam_init)
    _project_residual_norm(hmix, x_ref, mod_ref, wout_ref, vecs_ref, o_ref)


SCORES_AHEAD = 3
FINISH_AT = N_MAPS - 2


def _attn_latent_kernel(q_ref, kt_ref, v_ref, g_ref, x_ref, mod_ref, wout_ref, vecs_ref, dl_ref, o_ref,
                        outs_ref, *, lam_init, n_blocks):
    t = pl.program_id(0)

    @pl.when(t == 0)
    def _():
        outs_ref[...] = jnp.zeros(outs_ref.shape, outs_ref.dtype)

    def previous_mix():
        return _gated_mix([outs_ref[i] for i in range(N_MAPS)], g_ref, vecs_ref, dl_ref, lam_init)

    @pl.when(t < n_blocks)
    def _():
        pending = [_map_scores(i, q_ref, kt_ref) for i in range(SCORES_AHEAD)]
        hmix = previous_mix()
        p_cur = _probs(pending.pop(0))
        for i in range(N_MAPS):
            if i + SCORES_AHEAD < N_MAPS:
                pending.append(_map_scores(i + SCORES_AHEAD, q_ref, kt_ref))
            p_next = _probs(pending.pop(0)) if i + 1 < N_MAPS else None
            if i == FINISH_AT:
                _project_residual_norm(hmix, x_ref, mod_ref, wout_ref, vecs_ref, o_ref)
            outs_ref[i] = _map_values(i, p_cur, v_ref)
            p_cur = p_next

    @pl.when(t == n_blocks)
    def _():
        _project_residual_norm(previous_mix(), x_ref, mod_ref, wout_ref, vecs_ref, o_ref)


def _attn_latent_call(layer, q, kt, v, g, xl, mod, w_out_b, vecs, dl, lam_init):
    B, S, _ = xl.shape
    T = kt.shape[2]
    per_batch = S // ATTN_ROWS
    n_blocks = B * per_batch

    def cur(t):
        c = jnp.minimum(t, n_blocks - 1)
        return c // per_batch, c % per_batch

    def prev(t):
        p = jnp.maximum(t - 1, 0)
        return p // per_batch, p % per_batch

    kernel = functools.partial(_attn_latent_kernel, lam_init=lam_init, n_blocks=n_blocks)
    return pl.pallas_call(
        kernel,
        grid=(n_blocks + 1,),
        in_specs=[
            pl.BlockSpec((None, ATTN_ROWS, Q_COLS), lambda t: (*cur(t), 0)),
            pl.BlockSpec((None, KT_ROWS, T), lambda t: (cur(t)[0], 0, 0)),
            pl.BlockSpec((None, T, V_COLS), lambda t: (cur(t)[0], 0, 0)),
            pl.BlockSpec((None, ATTN_ROWS, MIX_WIDTH), lambda t: (*prev(t), 0)),
            pl.BlockSpec((None, ATTN_ROWS, D_MODEL), lambda t: (*prev(t), 0)),
            pl.BlockSpec((None, None, 1, 3 * D_MODEL), lambda t: (layer, prev(t)[0], 0, 0)),
            pl.BlockSpec((None, MIX_WIDTH, D_MODEL), lambda t: (layer, 0, 0), pipeline_mode=pl.Buffered(1)),
            pl.BlockSpec((None, 8, D_MODEL), lambda t: (layer, 0, 0)),
            pl.BlockSpec((None, 4, DIFF_QK_DIM), lambda t: (layer, 0, 0)),
        ],
        out_specs=pl.BlockSpec((None, ATTN_ROWS, D_MODEL), lambda t: (*prev(t), 0)),
        out_shape=jax.ShapeDtypeStruct((B, S, D_MODEL), jnp.float32),
        scratch_shapes=[pltpu.VMEM((N_MAPS, ATTN_ROWS, LANES), jnp.float32)],
        compiler_params=pltpu.CompilerParams(
            dimension_semantics=("arbitrary",), vmem_limit_bytes=VMEM_LIMIT),
        name=f"attn_latent_l{layer}",
    )(q, kt, v, g, xl, mod, w_out_b, vecs, dl)


def _attn_ctx_call(layer, q, kt, v, g, xc, mod, w_out_b, vecs, dl, lam_init):
    B = xc.shape[0]
    blk = (kt.shape[2] - CTX_LEN) // CTX_LEN
    kernel = functools.partial(_attn_ctx_kernel, lam_init=lam_init)
    return pl.pallas_call(
        kernel,
        grid=(B,),
        in_specs=[
            pl.BlockSpec((None, CTX_LEN, Q_COLS), lambda b: (b, blk, 0)),
            pl.BlockSpec((None, KT_ROWS, CTX_LEN), lambda b: (b, 0, blk)),
            pl.BlockSpec((None, CTX_LEN, V_COLS), lambda b: (b, blk, 0)),
            pl.BlockSpec((None, CTX_LEN, MIX_WIDTH), lambda b: (b, blk, 0)),
            pl.BlockSpec((None, CTX_LEN, D_MODEL), lambda b: (b, 0, 0)),
            pl.BlockSpec((None, None, 1, 3 * D_MODEL), lambda b: (layer, CTX_ROW, 0, 0)),
            pl.BlockSpec((None, MIX_WIDTH, D_MODEL), lambda b: (layer, 0, 0)),
            pl.BlockSpec((None, 8, D_MODEL), lambda b: (layer, 0, 0)),
            pl.BlockSpec((None, 4, DIFF_QK_DIM), lambda b: (layer, 0, 0)),
        ],
        out_specs=pl.BlockSpec((None, CTX_LEN, D_MODEL), lambda b: (b, 0, 0)),
        out_shape=jax.ShapeDtypeStruct((B, CTX_LEN, D_MODEL), jnp.float32),
        compiler_params=pltpu.CompilerParams(
            dimension_semantics=("arbitrary",), vmem_limit_bytes=VMEM_LIMIT),
        name=f"attn_ctx_l{layer}",
    )(q, kt, v, g, xc, mod, w_out_b, vecs, dl)


def _rope_tables(seq):
    rows_n = seq // GRID_W
    row = jnp.repeat(jnp.arange(rows_n, dtype=jnp.int32), GRID_W)
    col = jnp.tile(jnp.arange(GRID_W, dtype=jnp.int32), rows_n)

    def tables(pos, dim):
        freqs = ROPE_THETA ** (-jnp.arange(0, dim, 2, dtype=jnp.float32) / dim)
        ang = pos.astype(jnp.float32)[:, None] * freqs[None, :]
        return jnp.cos(ang), jnp.sin(ang)

    def head_tables(head_rot_dim):
        half = head_rot_dim // 2
        cr, sr = tables(row, half)
        cc, sc = tables(col, half)
        return (jnp.concatenate([cr, cr, cc, cc], axis=-1),
                jnp.concatenate([-sr, sr, -sc, sc], axis=-1))

    c64, s64 = head_tables(GQA_HEAD_DIM)
    c32, s32 = head_tables(DIFF_QK_DIM)
    ones, zeros = jnp.ones((seq, 64), jnp.float32), jnp.zeros((seq, 64), jnp.float32)
    perm_a, perm_d, perm_m = _pair_split_perms()
    cos_l = jnp.stack([jnp.tile(c64, (1, 2))[:, perm_a], jnp.tile(c32, (1, 4))[:, perm_d],
                       jnp.concatenate([ones, c32, ones[:, :32]], axis=-1)[:, perm_m]])
    sin_l = jnp.stack([jnp.tile(s64, (1, 2))[:, perm_a], jnp.tile(s32, (1, 4))[:, perm_d],
                       jnp.concatenate([zeros, s32, zeros[:, :32]], axis=-1)[:, perm_m]])
    cos_t = jnp.concatenate([cos_l, jnp.ones((3, CTX_LEN, LANES), jnp.float32)], axis=1)
    sin_t = jnp.concatenate([sin_l, jnp.zeros((3, CTX_LEN, LANES), jnp.float32)], axis=1)
    return cos_t, sin_t


def _pair_split_perms():
    def first_member(w, half):
        return w if w < half else 2 * half + (w - half)

    perm_a = np.zeros((LANES,), np.int64)
    perm_d = np.zeros((LANES,), np.int64)
    perm_m = np.full((LANES,), LANES - 1, np.int64)
    for lane in range(LANES):
        side, r = lane // 64, lane % 64
        perm_a[lane] = (r // 32) * 64 + first_member(r % 32, 16) + 16 * side
        perm_d[lane] = (r // 16) * 32 + first_member(r % 16, 8) + 8 * side
        if r < 16:
            perm_m[lane] = MLA_NOPE + first_member(r, 8) + 8 * side
        elif side == 0:
            perm_m[lane] = r - 16
        elif r < 32:
            perm_m[lane] = 48 + (r - 16)
    return perm_a, perm_d, perm_m


def _in_proj_columns():
    sizes = (384, 128, 128, 256, 256, 256, 192, 128, 32, 384, 256, 384)
    starts = np.concatenate([[0], np.cumsum(sizes)[:-1]])
    (s_gq, s_gk, s_gv, s_dq, s_dk, s_dv, s_mq, s_mkv, s_mkr, s_ga, _, _) = [int(s) for s in starts]
    cols = np.full((IN_COLS_P,), -1, np.int64)
    for c in range(3):
        cols[OFF_GQ + c * 128:OFF_GQ + c * 128 + 64] = s_gq + c * 64 + np.arange(64)
        cols[OFF_GQ + c * 128 + 64:OFF_GQ + (c + 1) * 128] = s_gq + (3 + c) * 64 + np.arange(64)
    cols[OFF_GK:OFF_GK + 128] = s_gk + np.arange(128)
    cols[OFF_GV:OFF_GV + 128] = s_gv + np.arange(128)
    cols[OFF_DQ:OFF_DQ + 256] = s_dq + np.arange(256)
    cols[OFF_DK:OFF_DK + 256] = s_dk + np.arange(256)
    cols[OFF_DV:OFF_DV + 256] = s_dv + np.arange(256)
    cols[OFF_GATE:OFF_GATE + 1024] = s_ga + np.arange(1024)
    cols[OFF_MKV:OFF_MKV + 128] = s_mkv + np.arange(128)
    cols[OFF_MQ:OFF_MQ + 192] = s_mq + np.arange(192)
    cols[OFF_MKR:OFF_MKR + 32] = s_mkr + np.concatenate([np.arange(8), 16 + np.arange(8),
                                                         8 + np.arange(8), 24 + np.arange(8)])
    perm_a, perm_d, _ = _pair_split_perms()
    for off, perm in ([(OFF_GQ + c * LANES, perm_a) for c in range(3)] + [(OFF_GK, perm_a)]
                      + [(o + c * LANES, perm_d) for o in (OFF_DQ, OFF_DK) for c in range(2)]):
        cols[off:off + LANES] = cols[off:off + LANES][perm]
    return cols


def _gather_cols(w, cols, dtype):
    wt = jnp.swapaxes(w, -1, -2)
    parts, start = [], 0
    for j in range(1, len(cols) + 1):
        run_ends = (j == len(cols) or (cols[j] < 0) != (cols[start] < 0)
                    or (cols[start] >= 0 and cols[j] != cols[j - 1] + 1))
        if run_ends:
            if cols[start] < 0:
                parts.append(jnp.zeros(wt.shape[:-2] + (j - start, wt.shape[-1]), wt.dtype))
            else:
                parts.append(wt[..., int(cols[start]):int(cols[j - 1]) + 1, :])
            start = j
    return jnp.swapaxes(jnp.concatenate(parts, axis=-2).astype(dtype), -1, -2)


def kernel(x, c, ctx, c_ctx, w_ada, b_ada, w_in, gqa_q_g, gqa_k_g, diff_lq1, diff_lk1, diff_lq2, diff_lk2,
           diff_subln_g, mla_q_g, w_uq, mla_kv_g, w_ukv, w_out, ln_g, ln_b):
    B, S, D = x.shape
    f32, bf16 = jnp.float32, jnp.bfloat16

    w_in_p = _gather_cols(w_in, _in_proj_columns(), bf16)
    uq_cols = np.full((MLA_HEADS * LANES,), -1, np.int64)
    ukv_cols = np.full((2 * MLA_HEADS * LANES,), -1, np.int64)
    perm_a, _, perm_m = _pair_split_perms()
    for hh in range(MLA_HEADS):
        uq_cols[hh * LANES:hh * LANES + 96] = hh * 96 + np.arange(96)
        ukv_cols[hh * LANES:hh * LANES + 64] = hh * 128 + np.arange(64)
        ukv_cols[(MLA_HEADS + hh) * LANES:(MLA_HEADS + hh) * LANES + 64] = hh * 128 + 64 + np.arange(64)
        for cols in (uq_cols, ukv_cols):
            cols[hh * LANES:(hh + 1) * LANES] = cols[hh * LANES:(hh + 1) * LANES][perm_m]
    w_uq_p = jnp.pad(_gather_cols(w_uq, uq_cols, bf16), ((0, 0), (0, 2 * LANES - MLA_Q_RANK), (0, 0)))
    w_ukv_p = _gather_cols(w_ukv, ukv_cols, bf16)
    w_out_b = w_out.astype(bf16)

    def row(vec, width):
        return jnp.pad(vec, ((0, 0), (0, width - vec.shape[-1])))[:, None, :]

    gains = jnp.concatenate([
        row(jnp.tile(gqa_q_g, (1, 2))[:, perm_a], 2 * LANES), row(jnp.tile(gqa_k_g, (1, 2))[:, perm_a], 2 * LANES),
        row(mla_q_g, 2 * LANES), row(mla_kv_g, 2 * LANES),
        jnp.zeros((DEPTH, 4, 2 * LANES), f32)], axis=1)
    vecs = jnp.concatenate([
        row(ln_g, D), row(ln_b, D), row(jnp.tile(diff_subln_g, (1, 2)), D),
        jnp.zeros((DEPTH, 5, D), f32)], axis=1)
    dl = jnp.stack([diff_lq1, diff_lk1, diff_lq2, diff_lk2], axis=1)
    cos_t, sin_t = _rope_tables(S)

    cc = jnp.concatenate([c, c_ctx[None, :], jnp.zeros((ADA_ROWS - B - 1, D), f32)], axis=0)
    mod = _ada_call(cc, w_ada, b_ada[:, None, :])[:, :, None, :]

    xl, xc = x, ctx
    for layer in range(DEPTH):
        lam_init = 0.8 - 0.6 * math.exp(-0.3 * layer)
        q, kt, v, g = _proj_call(layer, xl, xc, mod, w_in_p, gains, w_uq_p, w_ukv_p, cos_t, sin_t)
        xl_new = _attn_latent_call(layer, q, kt, v, g, xl, mod, w_out_b, vecs, dl, lam_init)
        if layer < DEPTH - 1:
            xc = _attn_ctx_call(layer, q, kt, v, g, xc, mod, w_out_b, vecs, dl, lam_init)
        xl = xl_new
    return xl
```

```python
import functools
import math

import numpy as np
import jax
import jax.numpy as jnp
from jax import lax
from jax.experimental import pallas as pl
from jax.experimental.pallas import tpu as pltpu

D_MODEL = 1024
DEPTH = 4
GRID_W = 64
CTX_LEN = 256
ROPE_THETA = 10000.0
EPS = 1e-6

GQA_HEADS = 6
GQA_KV_HEADS = 2
GQA_HEAD_DIM = 64
DIFF_HEADS = 4
DIFF_QK_DIM = 32
DIFF_V_DIM = 64
MLA_HEADS = 6
MLA_NOPE = 64
MLA_ROPE = 32
MLA_V = 64
MLA_Q_RANK = 192
MLA_KV_RANK = 128
MIX_WIDTH = 1024

DEEPNORM_ALPHA = (2.0 * DEPTH) ** 0.25
LOG2_E = math.log2(math.e)
GQA_QSCALE = GQA_HEAD_DIM ** -0.5 * LOG2_E
DIFF_QSCALE = DIFF_QK_DIM ** -0.5 * LOG2_E
MLA_QSCALE = (MLA_NOPE + MLA_ROPE) ** -0.5 * LOG2_E

LANES = 128
ROW_BLOCK = 256
ATTN_ROWS = 256
ADA_ROWS = 16
CTX_ROW = 8
ADA_COL_BLOCK = 512

OFF_GQ, OFF_GK, OFF_GV = 0, 384, 512
OFF_DQ, OFF_DK, OFF_DV = 640, 896, 1152
OFF_GATE = 1408
OFF_MKV = 2432
OFF_MQ = 2560
OFF_MKR = 2752
IN_COLS_P = 2816

Q_OFF_A, Q_OFF_D, Q_OFF_M = 0, 384, 640
Q_COLS = 1408
KT_OFF_A, KT_OFF_D, KT_OFF_M = 0, 128, 384
KT_ROWS = 1152
V_CH_A, V_CH_D, V_CH_M = 0, 2, 6
V_COLS = 12 * LANES

VMEM_LIMIT = 56 * 1024 * 1024


def _lane_iota(rows, cols=LANES):
    return lax.broadcasted_iota(jnp.int32, (rows, cols), 1)


def _rope(x, cos, sin):
    return x * cos + pltpu.roll(x, LANES // 2, 1) * sin


def _ada_kernel(cc_ref, w_ref, b_ref, o_ref):
    cc = cc_ref[...]
    a = (cc * jax.nn.sigmoid(cc)).astype(jnp.bfloat16)
    w = w_ref[...].astype(jnp.bfloat16)
    o_ref[...] = jnp.dot(a, w, preferred_element_type=jnp.float32) + b_ref[...]


def _ada_call(cc, w_ada, b_ada3):
    n_col = (3 * D_MODEL) // ADA_COL_BLOCK
    return pl.pallas_call(
        _ada_kernel,
        grid=(DEPTH, n_col),
        in_specs=[
            pl.BlockSpec((ADA_ROWS, D_MODEL), lambda l, j: (0, 0)),
            pl.BlockSpec((None, D_MODEL, ADA_COL_BLOCK), lambda l, j: (l, 0, j)),
            pl.BlockSpec((None, 1, ADA_COL_BLOCK), lambda l, j: (l, 0, j)),
        ],
        out_specs=pl.BlockSpec((None, ADA_ROWS, ADA_COL_BLOCK), lambda l, j: (l, 0, j)),
        out_shape=jax.ShapeDtypeStruct((DEPTH, ADA_ROWS, 3 * D_MODEL), jnp.float32),
        name="ada_modulation",
    )(cc, w_ada, b_ada3)


def _modulated_in_proj(is_ctx, xl_ref, xc_ref, mod_ref, win_ref, z_ref):
    x = jnp.where(is_ctx, xc_ref[...], xl_ref[...])
    mu = jnp.mean(x, axis=-1, keepdims=True)
    xc = x - mu
    var = jnp.mean(xc * xc, axis=-1, keepdims=True)
    mod = mod_ref[...]
    shift = mod[:, 0:D_MODEL]
    scale = mod[:, D_MODEL:2 * D_MODEL]
    h = (xc * lax.rsqrt(var + EPS) * (1.0 + scale) + shift).astype(jnp.bfloat16)
    z_ref[...] = jnp.dot(h, win_ref[...], preferred_element_type=jnp.float32)


def _proj_kernel(xl_ref, xc_ref, mod_ref, win_ref, gains_ref, wuq_ref, wukv_ref, cos_ref, sin_ref,
                 q_ref, kt_ref, v_ref, g_ref, z_even, z_odd, *, n_blocks, blocks_per_batch):
    t = pl.program_id(0)
    is_ctx = jnp.minimum(t, n_blocks - 1) % blocks_per_batch == blocks_per_batch - 1

    @pl.when(t == 0)
    def _():
        z_odd[...] = jnp.zeros(z_odd.shape, z_odd.dtype)

    @pl.when(t % 2 == 0)
    def _():
        _modulated_in_proj(is_ctx, xl_ref, xc_ref, mod_ref, win_ref, z_even)
        _split_groups(z_odd, gains_ref, wuq_ref, wukv_ref, cos_ref, sin_ref, q_ref, kt_ref, v_ref, g_ref)

    @pl.when(t % 2 == 1)
    def _():
        _modulated_in_proj(is_ctx, xl_ref, xc_ref, mod_ref, win_ref, z_odd)
        _split_groups(z_even, gains_ref, wuq_ref, wukv_ref, cos_ref, sin_ref, q_ref, kt_ref, v_ref, g_ref)


def _split_groups(z, gains_ref, wuq_ref, wukv_ref, cos_ref, sin_ref, q_ref, kt_ref, v_ref, g_ref):
    rows = z.shape[0]
    lane = _lane_iota(rows)
    lo = lane < 64
    one_at_64 = (lane == 64).astype(jnp.float32)

    gains = gains_ref[...]
    g_gq = gains[0:1, 0:LANES]
    g_gk = gains[1:2, 0:LANES]
    g_mq = gains[2:3, :]
    g_mkv = gains[3:4, 0:LANES]
    cos_a, sin_a = cos_ref[0], sin_ref[0]
    cos_d, sin_d = cos_ref[1], sin_ref[1]
    cos_m, sin_m = cos_ref[2], sin_ref[2]

    first_head = (lane % 64) < 32

    def head_rms(zc, gain):
        sq = zc * zc
        s_a = jnp.sum(jnp.where(first_head, sq, 0.0), axis=-1, keepdims=True)
        s_b = jnp.sum(jnp.where(first_head, 0.0, sq), axis=-1, keepdims=True)
        r = jnp.where(first_head, lax.rsqrt(s_a / GQA_HEAD_DIM + EPS), lax.rsqrt(s_b / GQA_HEAD_DIM + EPS))
        return zc * r * gain

    for c in range(3):
        zc = z[:, OFF_GQ + c * LANES:OFF_GQ + (c + 1) * LANES]
        y = _rope(head_rms(zc, g_gq), cos_a, sin_a) * GQA_QSCALE
        q_ref[:, Q_OFF_A + c * LANES:Q_OFF_A + (c + 1) * LANES] = y.astype(q_ref.dtype)
    yk = _rope(head_rms(z[:, OFF_GK:OFF_GK + LANES], g_gk), cos_a, sin_a)
    kt_ref[KT_OFF_A:KT_OFF_A + LANES, :] = yk.T.astype(kt_ref.dtype)

    def store_v_pair(zv, chunk):
        v_lo = jnp.where(lo, zv, one_at_64)
        v_hi = jnp.where(lo, pltpu.roll(zv, 64, 1), one_at_64)
        v_ref[:, chunk * LANES:(chunk + 1) * LANES] = v_lo.astype(v_ref.dtype)
        v_ref[:, (chunk + 1) * LANES:(chunk + 2) * LANES] = v_hi.astype(v_ref.dtype)

    store_v_pair(z[:, OFF_GV:OFF_GV + LANES], V_CH_A)

    for c in range(2):
        zq = z[:, OFF_DQ + c * LANES:OFF_DQ + (c + 1) * LANES]
        yq = _rope(zq, cos_d, sin_d) * DIFF_QSCALE
        q_ref[:, Q_OFF_D + c * LANES:Q_OFF_D + (c + 1) * LANES] = yq.astype(q_ref.dtype)
        zk = z[:, OFF_DK + c * LANES:OFF_DK + (c + 1) * LANES]
        yk = _rope(zk, cos_d, sin_d)
        kt_ref[KT_OFF_D + c * LANES:KT_OFF_D + (c + 1) * LANES, :] = yk.T.astype(kt_ref.dtype)
        store_v_pair(z[:, OFF_DV + c * LANES:OFF_DV + (c + 1) * LANES], V_CH_D + 2 * c)

    zg = z[:, OFF_GATE:OFF_GATE + MIX_WIDTH]
    g_ref[...] = (zg * jax.nn.sigmoid(zg)).astype(g_ref.dtype)

    ckv = z[:, OFF_MKV:OFF_MKV + MLA_KV_RANK]
    ms = jnp.mean(ckv * ckv, axis=-1, keepdims=True)
    cn = (ckv * lax.rsqrt(ms + EPS) * g_mkv).astype(jnp.bfloat16)
    kv = jnp.dot(cn, wukv_ref[...], preferred_element_type=jnp.float32)
    last = z[:, IN_COLS_P - LANES:IN_COLS_P]
    kpe = jnp.where(lane < 16, pltpu.roll(last, 64, 1),
                    jnp.where((lane >= 64) & (lane < 80), pltpu.roll(last, LANES - 16, 1), 0.0))
    kpe = _rope(kpe, cos_m, sin_m)
    n_m = MLA_HEADS * LANES
    for hh in range(MLA_HEADS):
        k_h = kv[:, hh * LANES:(hh + 1) * LANES] + kpe
        kt_ref[KT_OFF_M + hh * LANES:KT_OFF_M + (hh + 1) * LANES, :] = k_h.T.astype(kt_ref.dtype)
        v_h = kv[:, n_m + hh * LANES:n_m + (hh + 1) * LANES] + one_at_64
        v_ref[:, (V_CH_M + hh) * LANES:(V_CH_M + hh + 1) * LANES] = v_h.astype(v_ref.dtype)

    zq = z[:, OFF_MQ:OFF_MQ + 2 * LANES]
    lane2 = _lane_iota(rows, 2 * LANES)
    msq = jnp.sum(jnp.where(lane2 < MLA_Q_RANK, zq * zq, 0.0), axis=-1, keepdims=True) / MLA_Q_RANK
    qn = (zq * lax.rsqrt(msq + EPS) * g_mq).astype(jnp.bfloat16)
    qm = jnp.dot(qn, wuq_ref[...], preferred_element_type=jnp.float32)
    for hh in range(MLA_HEADS):
        yq = _rope(qm[:, hh * LANES:(hh + 1) * LANES], cos_m, sin_m) * MLA_QSCALE
        q_ref[:, Q_OFF_M + hh * LANES:Q_OFF_M + (hh + 1) * LANES] = yq.astype(q_ref.dtype)


def _proj_call(layer, xl, xc, mod, w_in_p, gains, w_uq_p, w_ukv_p, cos_t, sin_t):
    B, S, _ = xl.shape
    n_lat = S // ROW_BLOCK
    nblk = n_lat + 1
    T = S + CTX_LEN
    n_blocks = B * nblk

    def cur(t):
        c = jnp.minimum(t, n_blocks - 1)
        return c // nblk, c % nblk

    def prev(t):
        p = jnp.maximum(t - 1, 0)
        return p // nblk, p % nblk

    def mod_map(t):
        b, i = cur(t)
        return (layer, jnp.where(i == n_lat, CTX_ROW, b), 0, 0)

    kernel = functools.partial(_proj_kernel, n_blocks=n_blocks, blocks_per_batch=nblk)
    return pl.pallas_call(
        kernel,
        grid=(n_blocks + 1,),
        in_specs=[
            pl.BlockSpec((None, ROW_BLOCK, D_MODEL), lambda t: (cur(t)[0], jnp.minimum(cur(t)[1], n_lat - 1), 0)),
            pl.BlockSpec((None, CTX_LEN, D_MODEL), lambda t: (cur(t)[0], 0, 0)),
            pl.BlockSpec((None, None, 1, 3 * D_MODEL), mod_map),
            pl.BlockSpec((None, D_MODEL, IN_COLS_P), lambda t: (layer, 0, 0)),
            pl.BlockSpec((None, 8, 2 * LANES), lambda t: (layer, 0, 0)),
            pl.BlockSpec((None, 2 * LANES, MLA_HEADS * LANES), lambda t: (layer, 0, 0)),
            pl.BlockSpec((None, MLA_KV_RANK, 2 * MLA_HEADS * LANES), lambda t: (layer, 0, 0)),
            pl.BlockSpec((3, ROW_BLOCK, LANES), lambda t: (0, prev(t)[1], 0)),
            pl.BlockSpec((3, ROW_BLOCK, LANES), lambda t: (0, prev(t)[1], 0)),
        ],
        out_specs=[
            pl.BlockSpec((None, ROW_BLOCK, Q_COLS), lambda t: (*prev(t), 0)),
            pl.BlockSpec((None, KT_ROWS, ROW_BLOCK), lambda t: (prev(t)[0], 0, prev(t)[1])),
            pl.BlockSpec((None, ROW_BLOCK, V_COLS), lambda t: (*prev(t), 0)),
            pl.BlockSpec((None, ROW_BLOCK, MIX_WIDTH), lambda t: (*prev(t), 0)),
        ],
        out_shape=[
            jax.ShapeDtypeStruct((B, T, Q_COLS), jnp.bfloat16),
            jax.ShapeDtypeStruct((B, KT_ROWS, T), jnp.bfloat16),
            jax.ShapeDtypeStruct((B, T, V_COLS), jnp.bfloat16),
            jax.ShapeDtypeStruct((B, T, MIX_WIDTH), jnp.bfloat16),
        ],
        scratch_shapes=[pltpu.VMEM((ROW_BLOCK, IN_COLS_P), jnp.float32),
                        pltpu.VMEM((ROW_BLOCK, IN_COLS_P), jnp.float32)],
        compiler_params=pltpu.CompilerParams(
            dimension_semantics=("arbitrary",), vmem_limit_bytes=VMEM_LIMIT),
        name=f"proj_l{layer}",
    )(xl, xc, mod, w_in_p, gains, w_uq_p, w_ukv_p, cos_t, sin_t)


def _scores(qm, kt):
    return jnp.dot(qm, kt, preferred_element_type=jnp.float32)


def _probs(s):
    m = jnp.max(s, axis=-1, keepdims=True)
    return jnp.exp2(s - m).astype(jnp.bfloat16)


def _weighted_values(p, vext):
    o = jnp.dot(p, vext, preferred_element_type=jnp.float32)
    return o / o[:, 64:65]


def _softmax_maps():
    maps = []
    for hd in range(GQA_HEADS):
        c, half = hd % 3, hd // 3
        maps.append((Q_OFF_A + c * LANES, (32 * half, 32 * half + 32), KT_OFF_A, V_CH_A + half))
    for hd in range(DIFF_HEADS):
        c = hd // 2
        for j in (2 * (hd % 2), 2 * (hd % 2) + 1):
            maps.append((Q_OFF_D + c * LANES, (16 * j, 16 * j + 16), KT_OFF_D + c * LANES, V_CH_D + hd))
    for hd in range(MLA_HEADS):
        maps.append((Q_OFF_M + hd * LANES, None, KT_OFF_M + hd * LANES, V_CH_M + hd))
    return maps


N_MAPS = GQA_HEADS + 2 * DIFF_HEADS + MLA_HEADS


def _map_scores(i, q_ref, kt_ref):
    q_off, sel, kt_off, _ = _softmax_maps()[i]
    qc = q_ref[:, q_off:q_off + LANES]
    if sel is not None:
        lane = _lane_iota(qc.shape[0]) % 64
        qc = jnp.where((lane >= sel[0]) & (lane < sel[1]), qc, jnp.zeros((), qc.dtype))
    return _scores(qc, kt_ref[kt_off:kt_off + LANES, :])


def _map_values(i, p, v_ref):
    ch = _softmax_maps()[i][3]
    return _weighted_values(p, v_ref[:, ch * LANES:(ch + 1) * LANES])


def _gated_mix(outs, g_ref, vecs_ref, dl_ref, lam_init):
    rows = g_ref.shape[0]
    lo = _lane_iota(rows) < 64
    dl = dl_ref[...]
    lam = (jnp.exp(jnp.sum(dl[0:1] * dl[1:2], axis=-1, keepdims=True))
           - jnp.exp(jnp.sum(dl[2:3] * dl[3:4], axis=-1, keepdims=True)) + lam_init)
    subln = vecs_ref[2:3, 0:LANES] * (1.0 - lam_init)

    heads = outs[:GQA_HEADS]
    for hd in range(DIFF_HEADS):
        od = outs[GQA_HEADS + 2 * hd] - lam * outs[GQA_HEADS + 2 * hd + 1]
        ms = jnp.sum(jnp.where(lo, od * od, 0.0), axis=-1, keepdims=True) / DIFF_V_DIM
        heads.append(od * lax.rsqrt(ms + EPS) * subln)
    heads += outs[GQA_HEADS + 2 * DIFF_HEADS:]

    chunks = [jnp.where(lo, heads[2 * k], pltpu.roll(heads[2 * k + 1], 64, 1))
              for k in range(MIX_WIDTH // LANES)]
    hmix = jnp.concatenate(chunks, axis=-1) * g_ref[...].astype(jnp.float32)
    return hmix.astype(jnp.bfloat16)


def _project_residual_norm(hmix, x_ref, mod_ref, wout_ref, vecs_ref, o_ref):
    y = jnp.dot(hmix, wout_ref[...], preferred_element_type=jnp.float32)
    gate = mod_ref[...][:, 2 * D_MODEL:3 * D_MODEL]
    r = DEEPNORM_ALPHA * x_ref[...] + gate * y
    mu = jnp.mean(r, axis=-1, keepdims=True)
    rc = r - mu
    var = jnp.mean(rc * rc, axis=-1, keepdims=True)
    o_ref[...] = rc * lax.rsqrt(var + EPS) * vecs_ref[0:1] + vecs_ref[1:2]


def _attn_ctx_kernel(q_ref, kt_ref, v_ref, g_ref, x_ref, mod_ref, wout_ref, vecs_ref, dl_ref, o_ref, *, lam_init):
    all_p = [_probs(_map_scores(i, q_ref, kt_ref)) for i in range(N_MAPS)]
    outs = [_map_values(i, p, v_ref) for i, p in enumerate(all_p)]
    hmix = _gated_mix(outs, g_ref, vecs_ref, dl_ref, lam_init)
    _project_residual_norm(hmix, x_ref, mod_ref, wout_ref, vecs_ref, o_ref)


SCORES_AHEAD = 3
FINISH_AT = N_MAPS - 2


def _attn_latent_kernel(q_ref, kt_ref, v_ref, g_ref, x_ref, mod_ref, wout_ref, vecs_ref, dl_ref, o_ref,
                        outs_ref, *, lam_init, n_blocks):
    t = pl.program_id(0)

    @pl.when(t == 0)
    def _():
        outs_ref[...] = jnp.zeros(outs_ref.shape, outs_ref.dtype)

    def previous_mix():
        return _gated_mix([outs_ref[i] for i in range(N_MAPS)], g_ref, vecs_ref, dl_ref, lam_init)

    @pl.when(t < n_blocks)
    def _():
        pending = [_map_scores(i, q_ref, kt_ref) for i in range(SCORES_AHEAD)]
        hmix = previous_mix()
        p_cur = _probs(pending.pop(0))
        for i in range(N_MAPS):
            if i + SCORES_AHEAD < N_MAPS:
                pending.append(_map_scores(i + SCORES_AHEAD, q_ref, kt_ref))
            p_next = _probs(pending.pop(0)) if i + 1 < N_MAPS else None
            if i == FINISH_AT:
                _project_residual_norm(hmix, x_ref, mod_ref, wout_ref, vecs_ref, o_ref)
            outs_ref[i] = _map_values(i, p_cur, v_ref)
            p_cur = p_next

    @pl.when(t == n_blocks)
    def _():
        _project_residual_norm(previous_mix(), x_ref, mod_ref, wout_ref, vecs_ref, o_ref)


def _attn_latent_call(layer, q, kt, v, g, xl, mod, w_out_b, vecs, dl, lam_init):
    B, S, _ = xl.shape
    T = kt.shape[2]
    per_batch = S // ATTN_ROWS
    n_blocks = B * per_batch

    def cur(t):
        c = jnp.minimum(t, n_blocks - 1)
        return c // per_batch, c % per_batch

    def prev(t):
        p = jnp.maximum(t - 1, 0)
        return p // per_batch, p % per_batch

    kernel = functools.partial(_attn_latent_kernel, lam_init=lam_init, n_blocks=n_blocks)
    return pl.pallas_call(
        kernel,
        grid=(n_blocks + 1,),
        in_specs=[
            pl.BlockSpec((None, ATTN_ROWS, Q_COLS), lambda t: (*cur(t), 0)),
            pl.BlockSpec((None, KT_ROWS, T), lambda t: (cur(t)[0], 0, 0)),
            pl.BlockSpec((None, T, V_COLS), lambda t: (cur(t)[0], 0, 0)),
            pl.BlockSpec((None, ATTN_ROWS, MIX_WIDTH), lambda t: (*prev(t), 0)),
            pl.BlockSpec((None, ATTN_ROWS, D_MODEL), lambda t: (*prev(t), 0)),
            pl.BlockSpec((None, None, 1, 3 * D_MODEL), lambda t: (layer, prev(t)[0], 0, 0)),
            pl.BlockSpec((None, MIX_WIDTH, D_MODEL), lambda t: (layer, 0, 0), pipeline_mode=pl.Buffered(1)),
            pl.BlockSpec((None, 8, D_MODEL), lambda t: (layer, 0, 0)),
            pl.BlockSpec((None, 4, DIFF_QK_DIM), lambda t: (layer, 0, 0)),
        ],
        out_specs=pl.BlockSpec((None, ATTN_ROWS, D_MODEL), lambda t: (*prev(t), 0)),
        out_shape=jax.ShapeDtypeStruct((B, S, D_MODEL), jnp.float32),
        scratch_shapes=[pltpu.VMEM((N_MAPS, ATTN_ROWS, LANES), jnp.float32)],
        compiler_params=pltpu.CompilerParams(
            dimension_semantics=("arbitrary",), vmem_limit_bytes=VMEM_LIMIT),
        name=f"attn_latent_l{layer}",
    )(q, kt, v, g, xl, mod, w_out_b, vecs, dl)


def _attn_ctx_call(layer, q, kt, v, g, xc, mod, w_out_b, vecs, dl, lam_init):
    B = xc.shape[0]
    blk = (kt.shape[2] - CTX_LEN) // CTX_LEN
    kernel = functools.partial(_attn_ctx_kernel, lam_init=lam_init)
    return pl.pallas_call(
        kernel,
        grid=(B,),
        in_specs=[
            pl.BlockSpec((None, CTX_LEN, Q_COLS), lambda b: (b, blk, 0)),
            pl.BlockSpec((None, KT_ROWS, CTX_LEN), lambda b: (b, 0, blk)),
            pl.BlockSpec((None, CTX_LEN, V_COLS), lambda b: (b, blk, 0)),
            pl.BlockSpec((None, CTX_LEN, MIX_WIDTH), lambda b: (b, blk, 0)),
            pl.BlockSpec((None, CTX_LEN, D_MODEL), lambda b: (b, 0, 0)),
            pl.BlockSpec((None, None, 1, 3 * D_MODEL), lambda b: (layer, CTX_ROW, 0, 0)),
            pl.BlockSpec((None, MIX_WIDTH, D_MODEL), lambda b: (layer, 0, 0)),
            pl.BlockSpec((None, 8, D_MODEL), lambda b: (layer, 0, 0)),
            pl.BlockSpec((None, 4, DIFF_QK_DIM), lambda b: (layer, 0, 0)),
        ],
        out_specs=pl.BlockSpec((None, CTX_LEN, D_MODEL), lambda b: (b, 0, 0)),
        out_shape=jax.ShapeDtypeStruct((B, CTX_LEN, D_MODEL), jnp.float32),
        compiler_params=pltpu.CompilerParams(
            dimension_semantics=("arbitrary",), vmem_limit_bytes=VMEM_LIMIT),
        name=f"attn_ctx_l{layer}",
    )(q, kt, v, g, xc, mod, w_out_b, vecs, dl)


def _rope_tables(seq):
    rows_n = seq // GRID_W
    row = jnp.repeat(jnp.arange(rows_n, dtype=jnp.int32), GRID_W)
    col = jnp.tile(jnp.arange(GRID_W, dtype=jnp.int32), rows_n)

    def tables(pos, dim):
        freqs = ROPE_THETA ** (-jnp.arange(0, dim, 2, dtype=jnp.float32) / dim)
        ang = pos.astype(jnp.float32)[:, None] * freqs[None, :]
        return jnp.cos(ang), jnp.sin(ang)

    def head_tables(head_rot_dim):
        half = head_rot_dim // 2
        cr, sr = tables(row, half)
        cc, sc = tables(col, half)
        return (jnp.concatenate([cr, cr, cc, cc], axis=-1),
                jnp.concatenate([-sr, sr, -sc, sc], axis=-1))

    c64, s64 = head_tables(GQA_HEAD_DIM)
    c32, s32 = head_tables(DIFF_QK_DIM)
    ones, zeros = jnp.ones((seq, 64), jnp.float32), jnp.zeros((seq, 64), jnp.float32)
    perm_a, perm_d, perm_m = _pair_split_perms()
    cos_l = jnp.stack([jnp.tile(c64, (1, 2))[:, perm_a], jnp.tile(c32, (1, 4))[:, perm_d],
                       jnp.concatenate([ones, c32, ones[:, :32]], axis=-1)[:, perm_m]])
    sin_l = jnp.stack([jnp.tile(s64, (1, 2))[:, perm_a], jnp.tile(s32, (1, 4))[:, perm_d],
                       jnp.concatenate([zeros, s32, zeros[:, :32]], axis=-1)[:, perm_m]])
    cos_t = jnp.concatenate([cos_l, jnp.ones((3, CTX_LEN, LANES), jnp.float32)], axis=1)
    sin_t = jnp.concatenate([sin_l, jnp.zeros((3, CTX_LEN, LANES), jnp.float32)], axis=1)
    return cos_t, sin_t


def _pair_split_perms():
    def first_member(w, half):
        return w if w < half else 2 * half + (w - half)

    perm_a = np.zeros((LANES,), np.int64)
    perm_d = np.zeros((LANES,), np.int64)
    perm_m = np.full((LANES,), LANES - 1, np.int64)
    for lane in range(LANES):
        side, r = lane // 64, lane % 64
        perm_a[lane] = (r // 32) * 64 + first_member(r % 32, 16) + 16 * side
        perm_d[lane] = (r // 16) * 32 + first_member(r % 16, 8) + 8 * side
        if r < 16:
            perm_m[lane] = MLA_NOPE + first_member(r, 8) + 8 * side
        elif side == 0:
            perm_m[lane] = r - 16
        elif r < 32:
            perm_m[lane] = 48 + (r - 16)
    return perm_a, perm_d, perm_m


def _in_proj_columns():
    sizes = (384, 128, 128, 256, 256, 256, 192, 128, 32, 384, 256, 384)
    starts = np.concatenate([[0], np.cumsum(sizes)[:-1]])
    (s_gq, s_gk, s_gv, s_dq, s_dk, s_dv, s_mq, s_mkv, s_mkr, s_ga, _, _) = [int(s) for s in starts]
    cols = np.full((IN_COLS_P,), -1, np.int64)
    for c in range(3):
        cols[OFF_GQ + c * 128:OFF_GQ + c * 128 + 64] = s_gq + c * 64 + np.arange(64)
        cols[OFF_GQ + c * 128 + 64:OFF_GQ + (c + 1) * 128] = s_gq + (3 + c) * 64 + np.arange(64)
    cols[OFF_GK:OFF_GK + 128] = s_gk + np.arange(128)
    cols[OFF_GV:OFF_GV + 128] = s_gv + np.arange(128)
    cols[OFF_DQ:OFF_DQ + 256] = s_dq + np.arange(256)
    cols[OFF_DK:OFF_DK + 256] = s_dk + np.arange(256)
    cols[OFF_DV:OFF_DV + 256] = s_dv + np.arange(256)
    cols[OFF_GATE:OFF_GATE + 1024] = s_ga + np.arange(1024)
    cols[OFF_MKV:OFF_MKV + 128] = s_mkv + np.arange(128)
    cols[OFF_MQ:OFF_MQ + 192] = s_mq + np.arange(192)
    cols[OFF_MKR:OFF_MKR + 32] = s_mkr + np.concatenate([np.arange(8), 16 + np.arange(8),
                                                         8 + np.arange(8), 24 + np.arange(8)])
    perm_a, perm_d, _ = _pair_split_perms()
    for off, perm in ([(OFF_GQ + c * LANES, perm_a) for c in range(3)] + [(OFF_GK, perm_a)]
                      + [(o + c * LANES, perm_d) for o in (OFF_DQ, OFF_DK) for c in range(2)]):
        cols[off:off + LANES] = cols[off:off + LANES][perm]
    return cols


def _static_runs(cols):
    runs, start = [], 0
    for j in range(1, len(cols) + 1):
        if (j == len(cols) or (cols[j] < 0) != (cols[start] < 0)
                or (cols[start] >= 0 and cols[j] != cols[j - 1] + 1)):
            runs.append((int(cols[start]) if cols[start] >= 0 else -1, j - start))
            start = j
    return runs


def _in_proj_layout_kernel(wt_ref, o_ref):
    cols = _in_proj_columns()
    for j in range(IN_COLS_P // LANES):
        pieces = [jnp.zeros((n, wt_ref.shape[1]), jnp.float32) if src < 0 else wt_ref[src:src + n, :]
                  for src, n in _static_runs(cols[j * LANES:(j + 1) * LANES])]
        block = pieces[0] if len(pieces) == 1 else jnp.concatenate(pieces, axis=0)
        o_ref[:, j * LANES:(j + 1) * LANES] = block.T.astype(o_ref.dtype)


def _in_proj_layout_call(w_in):
    n_layers, d_in, n_src = w_in.shape
    return pl.pallas_call(
        _in_proj_layout_kernel,
        grid=(n_layers,),
        in_specs=[pl.BlockSpec((None, n_src, d_in), lambda l: (l, 0, 0))],
        out_specs=pl.BlockSpec((None, d_in, IN_COLS_P), lambda l: (l, 0, 0)),
        out_shape=jax.ShapeDtypeStruct((n_layers, d_in, IN_COLS_P), jnp.bfloat16),
        compiler_params=pltpu.CompilerParams(
            dimension_semantics=("arbitrary",), vmem_limit_bytes=VMEM_LIMIT),
        name="in_proj_layout",
    )(jnp.swapaxes(w_in, 1, 2))


def _gather_cols(w, cols, dtype):
    wt = jnp.swapaxes(w, -1, -2)
    parts, start = [], 0
    for j in range(1, len(cols) + 1):
        run_ends = (j == len(cols) or (cols[j] < 0) != (cols[start] < 0)
                    or (cols[start] >= 0 and cols[j] != cols[j - 1] + 1))
        if run_ends:
            if cols[start] < 0:
                parts.append(jnp.zeros(wt.shape[:-2] + (j - start, wt.shape[-1]), wt.dtype))
            else:
                parts.append(wt[..., int(cols[start]):int(cols[j - 1]) + 1, :])
            start = j
    return jnp.swapaxes(jnp.concatenate(parts, axis=-2).astype(dtype), -1, -2)


def kernel(x, c, ctx, c_ctx, w_ada, b_ada, w_in, gqa_q_g, gqa_k_g, diff_lq1, diff_lk1, diff_lq2, diff_lk2,
           diff_subln_g, mla_q_g, w_uq, mla_kv_g, w_ukv, w_out, ln_g, ln_b):
    B, S, D = x.shape
    f32, bf16 = jnp.float32, jnp.bfloat16

    w_in_p = _in_proj_layout_call(w_in)
    uq_cols = np.full((MLA_HEADS * LANES,), -1, np.int64)
    ukv_cols = np.full((2 * MLA_HEADS * LANES,), -1, np.int64)
    perm_a, _, perm_m = _pair_split_perms()
    for hh in range(MLA_HEADS):
        uq_cols[hh * LANES:hh * LANES + 96] = hh * 96 + np.arange(96)
        ukv_cols[hh * LANES:hh * LANES + 64] = hh * 128 + np.arange(64)
        ukv_cols[(MLA_HEADS + hh) * LANES:(MLA_HEADS + hh) * LANES + 64] = hh * 128 + 64 + np.arange(64)
        for cols in (uq_cols, ukv_cols):
            cols[hh * LANES:(hh + 1) * LANES] = cols[hh * LANES:(hh + 1) * LANES][perm_m]
    w_uq_p = jnp.pad(_gather_cols(w_uq, uq_cols, bf16), ((0, 0), (0, 2 * LANES - MLA_Q_RANK), (0, 0)))
    w_ukv_p = _gather_cols(w_ukv, ukv_cols, bf16)
    w_out_b = w_out.astype(bf16)

    def row(vec, width):
        return jnp.pad(vec, ((0, 0), (0, width - vec.shape[-1])))[:, None, :]

    gains = jnp.concatenate([
        row(jnp.tile(gqa_q_g, (1, 2))[:, perm_a], 2 * LANES), row(jnp.tile(gqa_k_g, (1, 2))[:, perm_a], 2 * LANES),
        row(mla_q_g, 2 * LANES), row(mla_kv_g, 2 * LANES),
        jnp.zeros((DEPTH, 4, 2 * LANES), f32)], axis=1)
    vecs = jnp.concatenate([
        row(ln_g, D), row(ln_b, D), row(jnp.tile(diff_subln_g, (1, 2)), D),
        jnp.zeros((DEPTH, 5, D), f32)], axis=1)
    dl = jnp.stack([diff_lq1, diff_lk1, diff_lq2, diff_lk2], axis=1)
    cos_t, sin_t = _rope_tables(S)

    cc = jnp.concatenate([c, c_ctx[None, :], jnp.zeros((ADA_ROWS - B - 1, D), f32)], axis=0)
    mod = _ada_call(cc, w_ada, b_ada[:, None, :])[:, :, None, :]

    xl, xc = x, ctx
    for layer in range(DEPTH):
        lam_init = 0.8 - 0.6 * math.exp(-0.3 * layer)
        q, kt, v, g = _proj_call(layer, xl, xc, mod, w_in_p, gains, w_uq_p, w_ukv_p, cos_t, sin_t)
        xl_new = _attn_latent_call(layer, q, kt, v, g, xl, mod, w_out_b, vecs, dl, lam_init)
        if layer < DEPTH - 1:
            xc = _attn_ctx_call(layer, q, kt, v, g, xc, mod, w_out_b, vecs, dl, lam_init)
        xl = xl_new
    return xl
```

```python
import functools
import math

import numpy as np
import jax
import jax.numpy as jnp
from jax import lax
from jax.experimental import pallas as pl
from jax.experimental.pallas import tpu as pltpu

D_MODEL = 1024
DEPTH = 4
GRID_W = 64
CTX_LEN = 256
ROPE_THETA = 10000.0
EPS = 1e-6

GQA_HEADS = 6
GQA_KV_HEADS = 2
GQA_HEAD_DIM = 64
DIFF_HEADS = 4
DIFF_QK_DIM = 32
DIFF_V_DIM = 64
MLA_HEADS = 6
MLA_NOPE = 64
MLA_ROPE = 32
MLA_V = 64
MLA_Q_RANK = 192
MLA_KV_RANK = 128
MIX_WIDTH = 1024

DEEPNORM_ALPHA = (2.0 * DEPTH) ** 0.25
LOG2_E = math.log2(math.e)
GQA_QSCALE = GQA_HEAD_DIM ** -0.5 * LOG2_E
DIFF_QSCALE = DIFF_QK_DIM ** -0.5 * LOG2_E
MLA_QSCALE = (MLA_NOPE + MLA_ROPE) ** -0.5 * LOG2_E

LANES = 128
ROW_BLOCK = 256
ATTN_ROWS = 256
ADA_ROWS = 16
CTX_ROW = 8
ADA_COL_BLOCK = 512

OFF_GQ, OFF_GK, OFF_GV = 0, 384, 512
OFF_DQ, OFF_DK, OFF_DV = 640, 896, 1152
OFF_GATE = 1408
OFF_MKV = 2432
OFF_MQ = 2560
OFF_MKR = 2752
IN_COLS_P = 2816

Q_OFF_A, Q_OFF_D, Q_OFF_M = 0, 384, 640
Q_COLS = 1408
KT_OFF_A, KT_OFF_D, KT_OFF_M = 0, 128, 384
KT_ROWS = 1152
V_CH_A, V_CH_D, V_CH_M = 0, 2, 6
V_COLS = 12 * LANES

VMEM_LIMIT = 56 * 1024 * 1024


def _lane_iota(rows, cols=LANES):
    return lax.broadcasted_iota(jnp.int32, (rows, cols), 1)


def _rope(x, cos, sin):
    return x * cos + pltpu.roll(x, LANES // 2, 1) * sin


def _ada_kernel(cc_ref, w_ref, b_ref, o_ref):
    cc = cc_ref[...]
    a = (cc * jax.nn.sigmoid(cc)).astype(jnp.bfloat16)
    w = w_ref[...].astype(jnp.bfloat16)
    o_ref[...] = jnp.dot(a, w, preferred_element_type=jnp.float32) + b_ref[...]


def _ada_call(cc, w_ada, b_ada3):
    n_col = (3 * D_MODEL) // ADA_COL_BLOCK
    return pl.pallas_call(
        _ada_kernel,
        grid=(DEPTH, n_col),
        in_specs=[
            pl.BlockSpec((ADA_ROWS, D_MODEL), lambda l, j: (0, 0)),
            pl.BlockSpec((None, D_MODEL, ADA_COL_BLOCK), lambda l, j: (l, 0, j)),
            pl.BlockSpec((None, 1, ADA_COL_BLOCK), lambda l, j: (l, 0, j)),
        ],
        out_specs=pl.BlockSpec((None, ADA_ROWS, ADA_COL_BLOCK), lambda l, j: (l, 0, j)),
        out_shape=jax.ShapeDtypeStruct((DEPTH, ADA_ROWS, 3 * D_MODEL), jnp.float32),
        name="ada_modulation",
    )(cc, w_ada, b_ada3)


def _modulated_in_proj(is_ctx, xl_ref, xc_ref, mod_ref, win_ref, z_ref):
    x = jnp.where(is_ctx, xc_ref[...], xl_ref[...])
    mu = jnp.mean(x, axis=-1, keepdims=True)
    xc = x - mu
    var = jnp.mean(xc * xc, axis=-1, keepdims=True)
    mod = mod_ref[...]
    shift = mod[:, 0:D_MODEL]
    scale = mod[:, D_MODEL:2 * D_MODEL]
    h = (xc * lax.rsqrt(var + EPS) * (1.0 + scale) + shift).astype(jnp.bfloat16)
    z_ref[...] = jnp.dot(h, win_ref[...], preferred_element_type=jnp.float32)


def _proj_kernel(xl_ref, xc_ref, mod_ref, win_ref, gains_ref, wuq_ref, wukv_ref, cos_ref, sin_ref,
                 q_ref, kt_ref, v_ref, g_ref, z_even, z_odd, *, n_blocks, blocks_per_batch):
    t = pl.program_id(0)
    is_ctx = jnp.minimum(t, n_blocks - 1) % blocks_per_batch == blocks_per_batch - 1

    @pl.when(t == 0)
    def _():
        z_odd[...] = jnp.zeros(z_odd.shape, z_odd.dtype)

    @pl.when(t % 2 == 0)
    def _():
        _modulated_in_proj(is_ctx, xl_ref, xc_ref, mod_ref, win_ref, z_even)
        _split_groups(z_odd, gains_ref, wuq_ref, wukv_ref, cos_ref, sin_ref, q_ref, kt_ref, v_ref, g_ref)

    @pl.when(t % 2 == 1)
    def _():
        _modulated_in_proj(is_ctx, xl_ref, xc_ref, mod_ref, win_ref, z_odd)
        _split_groups(z_even, gains_ref, wuq_ref, wukv_ref, cos_ref, sin_ref, q_ref, kt_ref, v_ref, g_ref)


def _split_groups(z, gains_ref, wuq_ref, wukv_ref, cos_ref, sin_ref, q_ref, kt_ref, v_ref, g_ref):
    rows = z.shape[0]
    lane = _lane_iota(rows)
    lo = lane < 64
    one_at_64 = (lane == 64).astype(jnp.float32)

    gains = gains_ref[...]
    g_gq = gains[0:1, 0:LANES]
    g_gk = gains[1:2, 0:LANES]
    g_mq = gains[2:3, :]
    g_mkv = gains[3:4, 0:LANES]
    cos_a, sin_a = cos_ref[0], sin_ref[0]
    cos_d, sin_d = cos_ref[1], sin_ref[1]
    cos_m, sin_m = cos_ref[2], sin_ref[2]

    first_head = (lane % 64) < 32

    def head_rms(zc, gain):
        sq = zc * zc
        s_a = jnp.sum(jnp.where(first_head, sq, 0.0), axis=-1, keepdims=True)
        s_b = jnp.sum(jnp.where(first_head, 0.0, sq), axis=-1, keepdims=True)
        r = jnp.where(first_head, lax.rsqrt(s_a / GQA_HEAD_DIM + EPS), lax.rsqrt(s_b / GQA_HEAD_DIM + EPS))
        return zc * r * gain

    for c in range(3):
        zc = z[:, OFF_GQ + c * LANES:OFF_GQ + (c + 1) * LANES]
        y = _rope(head_rms(zc, g_gq), cos_a, sin_a) * GQA_QSCALE
        q_ref[:, Q_OFF_A + c * LANES:Q_OFF_A + (c + 1) * LANES] = y.astype(q_ref.dtype)
    yk = _rope(head_rms(z[:, OFF_GK:OFF_GK + LANES], g_gk), cos_a, sin_a)
    kt_ref[KT_OFF_A:KT_OFF_A + LANES, :] = yk.T.astype(kt_ref.dtype)

    def store_v_pair(zv, chunk):
        v_lo = jnp.where(lo, zv, one_at_64)
        v_hi = jnp.where(lo, pltpu.roll(zv, 64, 1), one_at_64)
        v_ref[:, chunk * LANES:(chunk + 1) * LANES] = v_lo.astype(v_ref.dtype)
        v_ref[:, (chunk + 1) * LANES:(chunk + 2) * LANES] = v_hi.astype(v_ref.dtype)

    store_v_pair(z[:, OFF_GV:OFF_GV + LANES], V_CH_A)

    for c in range(2):
        zq = z[:, OFF_DQ + c * LANES:OFF_DQ + (c + 1) * LANES]
        yq = _rope(zq, cos_d, sin_d) * DIFF_QSCALE
        q_ref[:, Q_OFF_D + c * LANES:Q_OFF_D + (c + 1) * LANES] = yq.astype(q_ref.dtype)
        zk = z[:, OFF_DK + c * LANES:OFF_DK + (c + 1) * LANES]
        yk = _rope(zk, cos_d, sin_d)
        kt_ref[KT_OFF_D + c * LANES:KT_OFF_D + (c + 1) * LANES, :] = yk.T.astype(kt_ref.dtype)
        store_v_pair(z[:, OFF_DV + c * LANES:OFF_DV + (c + 1) * LANES], V_CH_D + 2 * c)

    zg = z[:, OFF_GATE:OFF_GATE + MIX_WIDTH]
    g_ref[...] = (zg * jax.nn.sigmoid(zg)).astype(g_ref.dtype)

    ckv = z[:, OFF_MKV:OFF_MKV + MLA_KV_RANK]
    ms = jnp.mean(ckv * ckv, axis=-1, keepdims=True)
    cn = (ckv * lax.rsqrt(ms + EPS) * g_mkv).astype(jnp.bfloat16)
    kv = jnp.dot(cn, wukv_ref[...], preferred_element_type=jnp.float32)
    last = z[:, IN_COLS_P - LANES:IN_COLS_P]
    kpe = jnp.where(lane < 16, pltpu.roll(last, 64, 1),
                    jnp.where((lane >= 64) & (lane < 80), pltpu.roll(last, LANES - 16, 1), 0.0))
    kpe = _rope(kpe, cos_m, sin_m)
    n_m = MLA_HEADS * LANES
    for hh in range(MLA_HEADS):
        k_h = kv[:, hh * LANES:(hh + 1) * LANES] + kpe
        kt_ref[KT_OFF_M + hh * LANES:KT_OFF_M + (hh + 1) * LANES, :] = k_h.T.astype(kt_ref.dtype)
        v_h = kv[:, n_m + hh * LANES:n_m + (hh + 1) * LANES] + one_at_64
        v_ref[:, (V_CH_M + hh) * LANES:(V_CH_M + hh + 1) * LANES] = v_h.astype(v_ref.dtype)

    zq = z[:, OFF_MQ:OFF_MQ + 2 * LANES]
    lane2 = _lane_iota(rows, 2 * LANES)
    msq = jnp.sum(jnp.where(lane2 < MLA_Q_RANK, zq * zq, 0.0), axis=-1, keepdims=True) / MLA_Q_RANK
    qn = (zq * lax.rsqrt(msq + EPS) * g_mq).astype(jnp.bfloat16)
    qm = jnp.dot(qn, wuq_ref[...], preferred_element_type=jnp.float32)
    for hh in range(MLA_HEADS):
        yq = _rope(qm[:, hh * LANES:(hh + 1) * LANES], cos_m, sin_m) * MLA_QSCALE
        q_ref[:, Q_OFF_M + hh * LANES:Q_OFF_M + (hh + 1) * LANES] = yq.astype(q_ref.dtype)


def _proj_call(layer, xl, xc, mod, w_in_p, gains, w_uq_p, w_ukv_p, cos_t, sin_t):
    B, S, _ = xl.shape
    n_lat = S // ROW_BLOCK
    nblk = n_lat + 1
    T = S + CTX_LEN
    n_blocks = B * nblk

    def cur(t):
        c = jnp.minimum(t, n_blocks - 1)
        return c // nblk, c % nblk

    def prev(t):
        p = jnp.maximum(t - 1, 0)
        return p // nblk, p % nblk

    def mod_map(t):
        b, i = cur(t)
        return (layer, jnp.where(i == n_lat, CTX_ROW, b), 0, 0)

    kernel = functools.partial(_proj_kernel, n_blocks=n_blocks, blocks_per_batch=nblk)
    return pl.pallas_call(
        kernel,
        grid=(n_blocks + 1,),
        in_specs=[
            pl.BlockSpec((None, ROW_BLOCK, D_MODEL), lambda t: (cur(t)[0], jnp.minimum(cur(t)[1], n_lat - 1), 0)),
            pl.BlockSpec((None, CTX_LEN, D_MODEL), lambda t: (cur(t)[0], 0, 0)),
            pl.BlockSpec((None, None, 1, 3 * D_MODEL), mod_map),
            pl.BlockSpec((None, D_MODEL, IN_COLS_P), lambda t: (layer, 0, 0)),
            pl.BlockSpec((None, 8, 2 * LANES), lambda t: (layer, 0, 0)),
            pl.BlockSpec((None, 2 * LANES, MLA_HEADS * LANES), lambda t: (layer, 0, 0)),
            pl.BlockSpec((None, MLA_KV_RANK, 2 * MLA_HEADS * LANES), lambda t: (layer, 0, 0)),
            pl.BlockSpec((3, ROW_BLOCK, LANES), lambda t: (0, prev(t)[1], 0)),
            pl.BlockSpec((3, ROW_BLOCK, LANES), lambda t: (0, prev(t)[1], 0)),
        ],
        out_specs=[
            pl.BlockSpec((None, ROW_BLOCK, Q_COLS), lambda t: (*prev(t), 0)),
            pl.BlockSpec((None, KT_ROWS, ROW_BLOCK), lambda t: (prev(t)[0], 0, prev(t)[1])),
            pl.BlockSpec((None, ROW_BLOCK, V_COLS), lambda t: (*prev(t), 0)),
            pl.BlockSpec((None, ROW_BLOCK, MIX_WIDTH), lambda t: (*prev(t), 0)),
        ],
        out_shape=[
            jax.ShapeDtypeStruct((B, T, Q_COLS), jnp.bfloat16),
            jax.ShapeDtypeStruct((B, KT_ROWS, T), jnp.bfloat16),
            jax.ShapeDtypeStruct((B, T, V_COLS), jnp.bfloat16),
            jax.ShapeDtypeStruct((B, T, MIX_WIDTH), jnp.bfloat16),
        ],
        scratch_shapes=[pltpu.VMEM((ROW_BLOCK, IN_COLS_P), jnp.float32),
                        pltpu.VMEM((ROW_BLOCK, IN_COLS_P), jnp.float32)],
        compiler_params=pltpu.CompilerParams(
            dimension_semantics=("arbitrary",), vmem_limit_bytes=VMEM_LIMIT),
        name=f"proj_l{layer}",
    )(xl, xc, mod, w_in_p, gains, w_uq_p, w_ukv_p, cos_t, sin_t)


def _scores(qm, kt):
    return jnp.dot(qm, kt, preferred_element_type=jnp.float32)


def _probs(s):
    m = jnp.max(s, axis=-1, keepdims=True)
    return jnp.exp2(s - m).astype(jnp.bfloat16)


def _weighted_values(p, vext):
    o = jnp.dot(p, vext, preferred_element_type=jnp.float32)
    return o / o[:, 64:65]


def _softmax_maps():
    maps = []
    for hd in range(GQA_HEADS):
        c, half = hd % 3, hd // 3
        maps.append((Q_OFF_A + c * LANES, (32 * half, 32 * half + 32), KT_OFF_A, V_CH_A + half))
    for hd in range(DIFF_HEADS):
        c = hd // 2
        for j in (2 * (hd % 2), 2 * (hd % 2) + 1):
            maps.append((Q_OFF_D + c * LANES, (16 * j, 16 * j + 16), KT_OFF_D + c * LANES, V_CH_D + hd))
    for hd in range(MLA_HEADS):
        maps.append((Q_OFF_M + hd * LANES, None, KT_OFF_M + hd * LANES, V_CH_M + hd))
    return maps


N_MAPS = GQA_HEADS + 2 * DIFF_HEADS + MLA_HEADS


def _map_scores(i, q_ref, kt_ref):
    q_off, sel, kt_off, _ = _softmax_maps()[i]
    qc = q_ref[:, q_off:q_off + LANES]
    if sel is not None:
        lane = _lane_iota(qc.shape[0]) % 64
        qc = jnp.where((lane >= sel[0]) & (lane < sel[1]), qc, jnp.zeros((), qc.dtype))
    return _scores(qc, kt_ref[kt_off:kt_off + LANES, :])


def _map_values(i, p, v_ref):
    ch = _softmax_maps()[i][3]
    return _weighted_values(p, v_ref[:, ch * LANES:(ch + 1) * LANES])


def _gated_mix(outs, g_ref, vecs_ref, dl_ref, lam_init):
    rows = g_ref.shape[0]
    lo = _lane_iota(rows) < 64
    dl = dl_ref[...]
    lam = (jnp.exp(jnp.sum(dl[0:1] * dl[1:2], axis=-1, keepdims=True))
           - jnp.exp(jnp.sum(dl[2:3] * dl[3:4], axis=-1, keepdims=True)) + lam_init)
    subln = vecs_ref[2:3, 0:LANES] * (1.0 - lam_init)

    heads = outs[:GQA_HEADS]
    for hd in range(DIFF_HEADS):
        od = outs[GQA_HEADS + 2 * hd] - lam * outs[GQA_HEADS + 2 * hd + 1]
        ms = jnp.sum(jnp.where(lo, od * od, 0.0), axis=-1, keepdims=True) / DIFF_V_DIM
        heads.append(od * lax.rsqrt(ms + EPS) * subln)
    heads += outs[GQA_HEADS + 2 * DIFF_HEADS:]

    chunks = [jnp.where(lo, heads[2 * k], pltpu.roll(heads[2 * k + 1], 64, 1))
              for k in range(MIX_WIDTH // LANES)]
    hmix = jnp.concatenate(chunks, axis=-1) * g_ref[...].astype(jnp.float32)
    return hmix.astype(jnp.bfloat16)


def _project_residual_norm(hmix, x_ref, mod_ref, wout_ref, vecs_ref, o_ref):
    y = jnp.dot(hmix, wout_ref[...], preferred_element_type=jnp.float32)
    gate = mod_ref[...][:, 2 * D_MODEL:3 * D_MODEL]
    r = DEEPNORM_ALPHA * x_ref[...] + gate * y
    mu = jnp.mean(r, axis=-1, keepdims=True)
    rc = r - mu
    var = jnp.mean(rc * rc, axis=-1, keepdims=True)
    o_ref[...] = rc * lax.rsqrt(var + EPS) * vecs_ref[0:1] + vecs_ref[1:2]


def _attn_ctx_kernel(q_ref, kt_ref, v_ref, g_ref, x_ref, mod_ref, wout_ref, vecs_ref, dl_ref, o_ref, *, lam_init):
    rows = x_ref.shape[0]
    lane64 = _lane_iota(rows) % 64
    maps = _softmax_maps()
    groups = {}
    for i, (_, _, kt_off, ch) in enumerate(maps):
        groups.setdefault((kt_off, ch), []).append(i)

    def stacked_q(members):
        parts = []
        for i in members:
            q_off, sel, _, _ = maps[i]
            qc = q_ref[:, q_off:q_off + LANES]
            if sel is not None:
                qc = jnp.where((lane64 >= sel[0]) & (lane64 < sel[1]), qc, jnp.zeros((), qc.dtype))
            parts.append(qc)
        return parts[0] if len(parts) == 1 else jnp.concatenate(parts, axis=0)

    all_s = [_scores(stacked_q(members), kt_ref[kt_off:kt_off + LANES, :])
             for (kt_off, _), members in groups.items()]
    all_p = [_probs(s) for s in all_s]
    outs = [None] * N_MAPS
    for ((_, ch), members), p in zip(groups.items(), all_p):
        o = _weighted_values(p, v_ref[:, ch * LANES:(ch + 1) * LANES])
        for n, i in enumerate(members):
            outs[i] = o[n * rows:(n + 1) * rows]
    hmix = _gated_mix(outs, g_ref, vecs_ref, dl_ref, lam_init)
    _project_residual_norm(hmix, x_ref, mod_ref, wout_ref, vecs_ref, o_ref)


SCORES_AHEAD = 3
FINISH_AT = N_MAPS - 2


def _attn_latent_kernel(q_ref, kt_ref, v_ref, g_ref, x_ref, mod_ref, wout_ref, vecs_ref, dl_ref, o_ref,
                        outs_ref, *, lam_init, n_blocks):
    t = pl.program_id(0)

    @pl.when(t == 0)
    def _():
        outs_ref[...] = jnp.zeros(outs_ref.shape, outs_ref.dtype)

    def previous_mix():
        return _gated_mix([outs_ref[i] for i in range(N_MAPS)], g_ref, vecs_ref, dl_ref, lam_init)

    @pl.when(t < n_blocks)
    def _():
        pending = [_map_scores(i, q_ref, kt_ref) for i in range(SCORES_AHEAD)]
        hmix = previous_mix()
        p_cur = _probs(pending.pop(0))
        for i in range(N_MAPS):
            if i + SCORES_AHEAD < N_MAPS:
                pending.append(_map_scores(i + SCORES_AHEAD, q_ref, kt_ref))
            p_next = _probs(pending.pop(0)) if i + 1 < N_MAPS else None
            if i == FINISH_AT:
                _project_residual_norm(hmix, x_ref, mod_ref, wout_ref, vecs_ref, o_ref)
            outs_ref[i] = _map_values(i, p_cur, v_ref)
            p_cur = p_next

    @pl.when(t == n_blocks)
    def _():
        _project_residual_norm(previous_mix(), x_ref, mod_ref, wout_ref, vecs_ref, o_ref)


def _attn_latent_call(layer, q, kt, v, g, xl, mod, w_out_b, vecs, dl, lam_init):
    B, S, _ = xl.shape
    T = kt.shape[2]
    per_batch = S // ATTN_ROWS
    n_blocks = B * per_batch

    def cur(t):
        c = jnp.minimum(t, n_blocks - 1)
        return c // per_batch, c % per_batch

    def prev(t):
        p = jnp.maximum(t - 1, 0)
        return p // per_batch, p % per_batch

    kernel = functools.partial(_attn_latent_kernel, lam_init=lam_init, n_blocks=n_blocks)
    return pl.pallas_call(
        kernel,
        grid=(n_blocks + 1,),
        in_specs=[
            pl.BlockSpec((None, ATTN_ROWS, Q_COLS), lambda t: (*cur(t), 0)),
            pl.BlockSpec((None, KT_ROWS, T), lambda t: (cur(t)[0], 0, 0)),
            pl.BlockSpec((None, T, V_COLS), lambda t: (cur(t)[0], 0, 0)),
            pl.BlockSpec((None, ATTN_ROWS, MIX_WIDTH), lambda t: (*prev(t), 0)),
            pl.BlockSpec((None, ATTN_ROWS, D_MODEL), lambda t: (*prev(t), 0)),
            pl.BlockSpec((None, None, 1, 3 * D_MODEL), lambda t: (layer, prev(t)[0], 0, 0)),
            pl.BlockSpec((None, MIX_WIDTH, D_MODEL), lambda t: (layer, 0, 0), pipeline_mode=pl.Buffered(1)),
            pl.BlockSpec((None, 8, D_MODEL), lambda t: (layer, 0, 0)),
            pl.BlockSpec((None, 4, DIFF_QK_DIM), lambda t: (layer, 0, 0)),
        ],
        out_specs=pl.BlockSpec((None, ATTN_ROWS, D_MODEL), lambda t: (*prev(t), 0)),
        out_shape=jax.ShapeDtypeStruct((B, S, D_MODEL), jnp.float32),
        scratch_shapes=[pltpu.VMEM((N_MAPS, ATTN_ROWS, LANES), jnp.float32)],
        compiler_params=pltpu.CompilerParams(
            dimension_semantics=("arbitrary",), vmem_limit_bytes=VMEM_LIMIT),
        name=f"attn_latent_l{layer}",
    )(q, kt, v, g, xl, mod, w_out_b, vecs, dl)


def _attn_ctx_call(layer, q, kt, v, g, xc, mod, w_out_b, vecs, dl, lam_init):
    B = xc.shape[0]
    blk = (kt.shape[2] - CTX_LEN) // CTX_LEN
    kernel = functools.partial(_attn_ctx_kernel, lam_init=lam_init)
    return pl.pallas_call(
        kernel,
        grid=(B,),
        in_specs=[
            pl.BlockSpec((None, CTX_LEN, Q_COLS), lambda b: (b, blk, 0)),
            pl.BlockSpec((None, KT_ROWS, CTX_LEN), lambda b: (b, 0, blk)),
            pl.BlockSpec((None, CTX_LEN, V_COLS), lambda b: (b, blk, 0)),
            pl.BlockSpec((None, CTX_LEN, MIX_WIDTH), lambda b: (b, blk, 0)),
            pl.BlockSpec((None, CTX_LEN, D_MODEL), lambda b: (b, 0, 0)),
            pl.BlockSpec((None, None, 1, 3 * D_MODEL), lambda b: (layer, CTX_ROW, 0, 0)),
            pl.BlockSpec((None, MIX_WIDTH, D_MODEL), lambda b: (layer, 0, 0)),
            pl.BlockSpec((None, 8, D_MODEL), lambda b: (layer, 0, 0)),
            pl.BlockSpec((None, 4, DIFF_QK_DIM), lambda b: (layer, 0, 0)),
        ],
        out_specs=pl.BlockSpec((None, CTX_LEN, D_MODEL), lambda b: (b, 0, 0)),
        out_shape=jax.ShapeDtypeStruct((B, CTX_LEN, D_MODEL), jnp.float32),
        compiler_params=pltpu.CompilerParams(
            dimension_semantics=("arbitrary",), vmem_limit_bytes=VMEM_LIMIT),
        name=f"attn_ctx_l{layer}",
    )(q, kt, v, g, xc, mod, w_out_b, vecs, dl)


def _rope_tables(seq):
    rows_n = seq // GRID_W
    row = jnp.repeat(jnp.arange(rows_n, dtype=jnp.int32), GRID_W)
    col = jnp.tile(jnp.arange(GRID_W, dtype=jnp.int32), rows_n)

    def tables(pos, dim):
        freqs = ROPE_THETA ** (-jnp.arange(0, dim, 2, dtype=jnp.float32) / dim)
        ang = pos.astype(jnp.float32)[:, None] * freqs[None, :]
        return jnp.cos(ang), jnp.sin(ang)

    def head_tables(head_rot_dim):
        half = head_rot_dim // 2
        cr, sr = tables(row, half)
        cc, sc = tables(col, half)
        return (jnp.concatenate([cr, cr, cc, cc], axis=-1),
                jnp.concatenate([-sr, sr, -sc, sc], axis=-1))

    c64, s64 = head_tables(GQA_HEAD_DIM)
    c32, s32 = head_tables(DIFF_QK_DIM)
    ones, zeros = jnp.ones((seq, 64), jnp.float32), jnp.zeros((seq, 64), jnp.float32)
    perm_a, perm_d, perm_m = _pair_split_perms()
    cos_l = jnp.stack([jnp.tile(c64, (1, 2))[:, perm_a], jnp.tile(c32, (1, 4))[:, perm_d],
                       jnp.concatenate([ones, c32, ones[:, :32]], axis=-1)[:, perm_m]])
    sin_l = jnp.stack([jnp.tile(s64, (1, 2))[:, perm_a], jnp.tile(s32, (1, 4))[:, perm_d],
                       jnp.concatenate([zeros, s32, zeros[:, :32]], axis=-1)[:, perm_m]])
    cos_t = jnp.concatenate([cos_l, jnp.ones((3, CTX_LEN, LANES), jnp.float32)], axis=1)
    sin_t = jnp.concatenate([sin_l, jnp.zeros((3, CTX_LEN, LANES), jnp.float32)], axis=1)
    return cos_t, sin_t


def _pair_split_perms():
    def first_member(w, half):
        return w if w < half else 2 * half + (w - half)

    perm_a = np.zeros((LANES,), np.int64)
    perm_d = np.zeros((LANES,), np.int64)
    perm_m = np.full((LANES,), LANES - 1, np.int64)
    for lane in range(LANES):
        side, r = lane // 64, lane % 64
        perm_a[lane] = (r // 32) * 64 + first_member(r % 32, 16) + 16 * side
        perm_d[lane] = (r // 16) * 32 + first_member(r % 16, 8) + 8 * side
        if r < 16:
            perm_m[lane] = MLA_NOPE + first_member(r, 8) + 8 * side
        elif side == 0:
            perm_m[lane] = r - 16
        elif r < 32:
            perm_m[lane] = 48 + (r - 16)
    return perm_a, perm_d, perm_m


def _in_proj_columns():
    sizes = (384, 128, 128, 256, 256, 256, 192, 128, 32, 384, 256, 384)
    starts = np.concatenate([[0], np.cumsum(sizes)[:-1]])
    (s_gq, s_gk, s_gv, s_dq, s_dk, s_dv, s_mq, s_mkv, s_mkr, s_ga, _, _) = [int(s) for s in starts]
    cols = np.full((IN_COLS_P,), -1, np.int64)
    for c in range(3):
        cols[OFF_GQ + c * 128:OFF_GQ + c * 128 + 64] = s_gq + c * 64 + np.arange(64)
        cols[OFF_GQ + c * 128 + 64:OFF_GQ + (c + 1) * 128] = s_gq + (3 + c) * 64 + np.arange(64)
    cols[OFF_GK:OFF_GK + 128] = s_gk + np.arange(128)
    cols[OFF_GV:OFF_GV + 128] = s_gv + np.arange(128)
    cols[OFF_DQ:OFF_DQ + 256] = s_dq + np.arange(256)
    cols[OFF_DK:OFF_DK + 256] = s_dk + np.arange(256)
    cols[OFF_DV:OFF_DV + 256] = s_dv + np.arange(256)
    cols[OFF_GATE:OFF_GATE + 1024] = s_ga + np.arange(1024)
    cols[OFF_MKV:OFF_MKV + 128] = s_mkv + np.arange(128)
    cols[OFF_MQ:OFF_MQ + 192] = s_mq + np.arange(192)
    cols[OFF_MKR:OFF_MKR + 32] = s_mkr + np.concatenate([np.arange(8), 16 + np.arange(8),
                                                         8 + np.arange(8), 24 + np.arange(8)])
    perm_a, perm_d, _ = _pair_split_perms()
    for off, perm in ([(OFF_GQ + c * LANES, perm_a) for c in range(3)] + [(OFF_GK, perm_a)]
                      + [(o + c * LANES, perm_d) for o in (OFF_DQ, OFF_DK) for c in range(2)]):
        cols[off:off + LANES] = cols[off:off + LANES][perm]
    return cols


def _static_runs(cols):
    runs, start = [], 0
    for j in range(1, len(cols) + 1):
        if (j == len(cols) or (cols[j] < 0) != (cols[start] < 0)
                or (cols[start] >= 0 and cols[j] != cols[j - 1] + 1)):
            runs.append((int(cols[start]) if cols[start] >= 0 else -1, j - start))
            start = j
    return runs


def _in_proj_layout_kernel(wt_ref, o_ref):
    cols = _in_proj_columns()
    for j in range(IN_COLS_P // LANES):
        pieces = [jnp.zeros((n, wt_ref.shape[1]), jnp.float32) if src < 0 else wt_ref[src:src + n, :]
                  for src, n in _static_runs(cols[j * LANES:(j + 1) * LANES])]
        block = pieces[0] if len(pieces) == 1 else jnp.concatenate(pieces, axis=0)
        o_ref[:, j * LANES:(j + 1) * LANES] = block.T.astype(o_ref.dtype)


def _in_proj_layout_call(w_in):
    n_layers, d_in, n_src = w_in.shape
    return pl.pallas_call(
        _in_proj_layout_kernel,
        grid=(n_layers,),
        in_specs=[pl.BlockSpec((None, n_src, d_in), lambda l: (l, 0, 0))],
        out_specs=pl.BlockSpec((None, d_in, IN_COLS_P), lambda l: (l, 0, 0)),
        out_shape=jax.ShapeDtypeStruct((n_layers, d_in, IN_COLS_P), jnp.bfloat16),
        compiler_params=pltpu.CompilerParams(
            dimension_semantics=("arbitrary",), vmem_limit_bytes=VMEM_LIMIT),
        name="in_proj_layout",
    )(jnp.swapaxes(w_in, 1, 2))


def _gather_cols(w, cols, dtype):
    wt = jnp.swapaxes(w, -1, -2)
    parts, start = [], 0
    for j in range(1, len(cols) + 1):
        run_ends = (j == len(cols) or (cols[j] < 0) != (cols[start] < 0)
                    or (cols[start] >= 0 and cols[j] != cols[j - 1] + 1))
        if run_ends:
            if cols[start] < 0:
                parts.append(jnp.zeros(wt.shape[:-2] + (j - start, wt.shape[-1]), wt.dtype))
            else:
                parts.append(wt[..., int(cols[start]):int(cols[j - 1]) + 1, :])
            start = j
    return jnp.swapaxes(jnp.concatenate(parts, axis=-2).astype(dtype), -1, -2)


def kernel(x, c, ctx, c_ctx, w_ada, b_ada, w_in, gqa_q_g, gqa_k_g, diff_lq1, diff_lk1, diff_lq2, diff_lk2,
           diff_subln_g, mla_q_g, w_uq, mla_kv_g, w_ukv, w_out, ln_g, ln_b):
    B, S, D = x.shape
    f32, bf16 = jnp.float32, jnp.bfloat16

    w_in_p = _in_proj_layout_call(w_in)
    uq_cols = np.full((MLA_HEADS * LANES,), -1, np.int64)
    ukv_cols = np.full((2 * MLA_HEADS * LANES,), -1, np.int64)
    perm_a, _, perm_m = _pair_split_perms()
    for hh in range(MLA_HEADS):
        uq_cols[hh * LANES:hh * LANES + 96] = hh * 96 + np.arange(96)
        ukv_cols[hh * LANES:hh * LANES + 64] = hh * 128 + np.arange(64)
        ukv_cols[(MLA_HEADS + hh) * LANES:(MLA_HEADS + hh) * LANES + 64] = hh * 128 + 64 + np.arange(64)
        for cols in (uq_cols, ukv_cols):
            cols[hh * LANES:(hh + 1) * LANES] = cols[hh * LANES:(hh + 1) * LANES][perm_m]
    w_uq_p = jnp.pad(_gather_cols(w_uq, uq_cols, bf16), ((0, 0), (0, 2 * LANES - MLA_Q_RANK), (0, 0)))
    w_ukv_p = _gather_cols(w_ukv, ukv_cols, bf16)
    w_out_b = w_out.astype(bf16)

    def row(vec, width):
        return jnp.pad(vec, ((0, 0), (0, width - vec.shape[-1])))[:, None, :]

    gains = jnp.concatenate([
        row(jnp.tile(gqa_q_g, (1, 2))[:, perm_a], 2 * LANES), row(jnp.tile(gqa_k_g, (1, 2))[:, perm_a], 2 * LANES),
        row(mla_q_g, 2 * LANES), row(mla_kv_g, 2 * LANES),
        jnp.zeros((DEPTH, 4, 2 * LANES), f32)], axis=1)
    vecs = jnp.concatenate([
        row(ln_g, D), row(ln_b, D), row(jnp.tile(diff_subln_g, (1, 2)), D),
        jnp.zeros((DEPTH, 5, D), f32)], axis=1)
    dl = jnp.stack([diff_lq1, diff_lk1, diff_lq2, diff_lk2], axis=1)
    cos_t, sin_t = _rope_tables(S)

    cc = jnp.concatenate([c, c_ctx[None, :], jnp.zeros((ADA_ROWS - B - 1, D), f32)], axis=0)
    mod = _ada_call(cc, w_ada, b_ada[:, None, :])[:, :, None, :]

    xl, xc = x, ctx
    for layer in range(DEPTH):
        lam_init = 0.8 - 0.6 * math.exp(-0.3 * layer)
        q, kt, v, g = _proj_call(layer, xl, xc, mod, w_in_p, gains, w_uq_p, w_ukv_p, cos_t, sin_t)
        xl_new = _attn_latent_call(layer, q, kt, v, g, xl, mod, w_out_b, vecs, dl, lam_init)
        if layer < DEPTH - 1:
            xc = _attn_ctx_call(layer, q, kt, v, g, xc, mod, w_out_b, vecs, dl, lam_init)
        xl = xl_new
    return xl
```

```python
import functools
import math

import numpy as np
import jax
import jax.numpy as jnp
from jax import lax
from jax.experimental import pallas as pl
from jax.experimental.pallas import tpu as pltpu

D_MODEL = 1024
DEPTH = 4
GRID_W = 64
CTX_LEN = 256
ROPE_THETA = 10000.0
EPS = 1e-6

GQA_HEADS = 6
GQA_KV_HEADS = 2
GQA_HEAD_DIM = 64
DIFF_HEADS = 4
DIFF_QK_DIM = 32
DIFF_V_DIM = 64
MLA_HEADS = 6
MLA_NOPE = 64
MLA_ROPE = 32
MLA_V = 64
MLA_Q_RANK = 192
MLA_KV_RANK = 128
MIX_WIDTH = 1024

DEEPNORM_ALPHA = (2.0 * DEPTH) ** 0.25
LOG2_E = math.log2(math.e)
GQA_QSCALE = GQA_HEAD_DIM ** -0.5 * LOG2_E
DIFF_QSCALE = DIFF_QK_DIM ** -0.5 * LOG2_E
MLA_QSCALE = (MLA_NOPE + MLA_ROPE) ** -0.5 * LOG2_E

LANES = 128
ROW_BLOCK = 256
ATTN_ROWS = 256
ADA_ROWS = 16
CTX_ROW = 8
ADA_COL_BLOCK = 1536

OFF_GQ, OFF_GK, OFF_GV = 0, 384, 512
OFF_DQ, OFF_DK, OFF_DV = 640, 896, 1152
OFF_GATE = 1408
OFF_MKV = 2432
OFF_MQ = 2560
OFF_MKR = 2752
IN_COLS_P = 2816

Q_OFF_A, Q_OFF_D, Q_OFF_M = 0, 384, 640
Q_COLS = 1408
KT_OFF_A, KT_OFF_D, KT_OFF_M = 0, 128, 384
KT_ROWS = 1152
V_CH_A, V_CH_D, V_CH_M = 0, 2, 6
V_COLS = 12 * LANES

VMEM_LIMIT = 56 * 1024 * 1024


def _lane_iota(rows, cols=LANES):
    return lax.broadcasted_iota(jnp.int32, (rows, cols), 1)


def _rope(x, cos, sin):
    return x * cos + pltpu.roll(x, LANES // 2, 1) * sin


def _ada_kernel(cc_ref, w_ref, b_ref, o_ref):
    cc = cc_ref[...]
    a = (cc * jax.nn.sigmoid(cc)).astype(jnp.bfloat16)
    w = w_ref[...].astype(jnp.bfloat16)
    o_ref[...] = jnp.dot(a, w, preferred_element_type=jnp.float32) + b_ref[...]


def _ada_call(cc, w_ada, b_ada3):
    n_col = (3 * D_MODEL) // ADA_COL_BLOCK
    return pl.pallas_call(
        _ada_kernel,
        grid=(DEPTH, n_col),
        in_specs=[
            pl.BlockSpec((ADA_ROWS, D_MODEL), lambda l, j: (0, 0)),
            pl.BlockSpec((None, D_MODEL, ADA_COL_BLOCK), lambda l, j: (l, 0, j)),
            pl.BlockSpec((None, 1, ADA_COL_BLOCK), lambda l, j: (l, 0, j)),
        ],
        out_specs=pl.BlockSpec((None, ADA_ROWS, ADA_COL_BLOCK), lambda l, j: (l, 0, j)),
        out_shape=jax.ShapeDtypeStruct((DEPTH, ADA_ROWS, 3 * D_MODEL), jnp.float32),
        compiler_params=pltpu.CompilerParams(
            dimension_semantics=("arbitrary", "arbitrary"), vmem_limit_bytes=VMEM_LIMIT),
        name="ada_modulation",
    )(cc, w_ada, b_ada3)


def _modulated_in_proj(is_ctx, xl_ref, xc_ref, mod_ref, win_ref, z_ref):
    x = jnp.where(is_ctx, xc_ref[...], xl_ref[...])
    mu = jnp.mean(x, axis=-1, keepdims=True)
    xc = x - mu
    var = jnp.mean(xc * xc, axis=-1, keepdims=True)
    mod = mod_ref[...]
    shift = mod[:, 0:D_MODEL]
    scale = mod[:, D_MODEL:2 * D_MODEL]
    h = (xc * lax.rsqrt(var + EPS) * (1.0 + scale) + shift).astype(jnp.bfloat16)
    z_ref[...] = jnp.dot(h, win_ref[...], preferred_element_type=jnp.float32)


def _proj_kernel(xl_ref, xc_ref, mod_ref, win_ref, gains_ref, wuq_ref, wukv_ref, cos_ref, sin_ref,
                 q_ref, kt_ref, v_ref, g_ref, z_even, z_odd, *, n_blocks, blocks_per_batch):
    t = pl.program_id(0)
    is_ctx = jnp.minimum(t, n_blocks - 1) % blocks_per_batch == blocks_per_batch - 1

    @pl.when(t == 0)
    def _():
        z_odd[...] = jnp.zeros(z_odd.shape, z_odd.dtype)

    @pl.when(t % 2 == 0)
    def _():
        _modulated_in_proj(is_ctx, xl_ref, xc_ref, mod_ref, win_ref, z_even)
        _split_groups(z_odd, gains_ref, wuq_ref, wukv_ref, cos_ref, sin_ref, q_ref, kt_ref, v_ref, g_ref)

    @pl.when(t % 2 == 1)
    def _():
        _modulated_in_proj(is_ctx, xl_ref, xc_ref, mod_ref, win_ref, z_odd)
        _split_groups(z_even, gains_ref, wuq_ref, wukv_ref, cos_ref, sin_ref, q_ref, kt_ref, v_ref, g_ref)


def _split_groups(z, gains_ref, wuq_ref, wukv_ref, cos_ref, sin_ref, q_ref, kt_ref, v_ref, g_ref):
    rows = z.shape[0]
    lane = _lane_iota(rows)
    lo = lane < 64
    one_at_64 = (lane == 64).astype(jnp.float32)

    gains = gains_ref[...]
    g_gq = gains[0:1, 0:LANES]
    g_gk = gains[1:2, 0:LANES]
    g_mq = gains[2:3, :]
    g_mkv = gains[3:4, 0:LANES]
    cos_a, sin_a = cos_ref[0], sin_ref[0]
    cos_d, sin_d = cos_ref[1], sin_ref[1]
    cos_m, sin_m = cos_ref[2], sin_ref[2]

    first_head = (lane % 64) < 32

    def head_rms(zc, gain):
        sq = zc * zc
        s_a = jnp.sum(jnp.where(first_head, sq, 0.0), axis=-1, keepdims=True)
        s_b = jnp.sum(jnp.where(first_head, 0.0, sq), axis=-1, keepdims=True)
        r = jnp.where(first_head, lax.rsqrt(s_a / GQA_HEAD_DIM + EPS), lax.rsqrt(s_b / GQA_HEAD_DIM + EPS))
        return zc * r * gain

    for c in range(3):
        zc = z[:, OFF_GQ + c * LANES:OFF_GQ + (c + 1) * LANES]
        y = _rope(head_rms(zc, g_gq), cos_a, sin_a) * GQA_QSCALE
        q_ref[:, Q_OFF_A + c * LANES:Q_OFF_A + (c + 1) * LANES] = y.astype(q_ref.dtype)
    yk = _rope(head_rms(z[:, OFF_GK:OFF_GK + LANES], g_gk), cos_a, sin_a)
    kt_ref[KT_OFF_A:KT_OFF_A + LANES, :] = yk.T.astype(kt_ref.dtype)

    def store_v_pair(zv, chunk):
        v_lo = jnp.where(lo, zv, one_at_64)
        v_hi = jnp.where(lo, pltpu.roll(zv, 64, 1), one_at_64)
        v_ref[:, chunk * LANES:(chunk + 1) * LANES] = v_lo.astype(v_ref.dtype)
        v_ref[:, (chunk + 1) * LANES:(chunk + 2) * LANES] = v_hi.astype(v_ref.dtype)

    store_v_pair(z[:, OFF_GV:OFF_GV + LANES], V_CH_A)

    for c in range(2):
        zq = z[:, OFF_DQ + c * LANES:OFF_DQ + (c + 1) * LANES]
        yq = _rope(zq, cos_d, sin_d) * DIFF_QSCALE
        q_ref[:, Q_OFF_D + c * LANES:Q_OFF_D + (c + 1) * LANES] = yq.astype(q_ref.dtype)
        zk = z[:, OFF_DK + c * LANES:OFF_DK + (c + 1) * LANES]
        yk = _rope(zk, cos_d, sin_d)
        kt_ref[KT_OFF_D + c * LANES:KT_OFF_D + (c + 1) * LANES, :] = yk.T.astype(kt_ref.dtype)
        store_v_pair(z[:, OFF_DV + c * LANES:OFF_DV + (c + 1) * LANES], V_CH_D + 2 * c)

    zg = z[:, OFF_GATE:OFF_GATE + MIX_WIDTH]
    g_ref[...] = (zg * jax.nn.sigmoid(zg)).astype(g_ref.dtype)

    ckv = z[:, OFF_MKV:OFF_MKV + MLA_KV_RANK]
    ms = jnp.mean(ckv * ckv, axis=-1, keepdims=True)
    cn = (ckv * lax.rsqrt(ms + EPS) * g_mkv).astype(jnp.bfloat16)
    kv = jnp.dot(cn, wukv_ref[...], preferred_element_type=jnp.float32)
    last = z[:, IN_COLS_P - LANES:IN_COLS_P]
    kpe = jnp.where(lane < 16, pltpu.roll(last, 64, 1),
                    jnp.where((lane >= 64) & (lane < 80), pltpu.roll(last, LANES - 16, 1), 0.0))
    kpe = _rope(kpe, cos_m, sin_m)
    n_m = MLA_HEADS * LANES
    for hh in range(MLA_HEADS):
        k_h = kv[:, hh * LANES:(hh + 1) * LANES] + kpe
        kt_ref[KT_OFF_M + hh * LANES:KT_OFF_M + (hh + 1) * LANES, :] = k_h.T.astype(kt_ref.dtype)
        v_h = kv[:, n_m + hh * LANES:n_m + (hh + 1) * LANES] + one_at_64
        v_ref[:, (V_CH_M + hh) * LANES:(V_CH_M + hh + 1) * LANES] = v_h.astype(v_ref.dtype)

    zq = z[:, OFF_MQ:OFF_MQ + 2 * LANES]
    lane2 = _lane_iota(rows, 2 * LANES)
    msq = jnp.sum(jnp.where(lane2 < MLA_Q_RANK, zq * zq, 0.0), axis=-1, keepdims=True) / MLA_Q_RANK
    qn = (zq * lax.rsqrt(msq + EPS) * g_mq).astype(jnp.bfloat16)
    qm = jnp.dot(qn, wuq_ref[...], preferred_element_type=jnp.float32)
    for hh in range(MLA_HEADS):
        yq = _rope(qm[:, hh * LANES:(hh + 1) * LANES], cos_m, sin_m) * MLA_QSCALE
        q_ref[:, Q_OFF_M + hh * LANES:Q_OFF_M + (hh + 1) * LANES] = yq.astype(q_ref.dtype)


def _proj_call(layer, xl, xc, mod, w_in_p, gains, w_uq_p, w_ukv_p, cos_t, sin_t):
    B, S, _ = xl.shape
    n_lat = S // ROW_BLOCK
    nblk = n_lat + 1
    T = S + CTX_LEN
    n_blocks = B * nblk

    def cur(t):
        c = jnp.minimum(t, n_blocks - 1)
        return c // nblk, c % nblk

    def prev(t):
        p = jnp.maximum(t - 1, 0)
        return p // nblk, p % nblk

    def mod_map(t):
        b, i = cur(t)
        return (layer, jnp.where(i == n_lat, CTX_ROW, b), 0, 0)

    kernel = functools.partial(_proj_kernel, n_blocks=n_blocks, blocks_per_batch=nblk)
    return pl.pallas_call(
        kernel,
        grid=(n_blocks + 1,),
        in_specs=[
            pl.BlockSpec((None, ROW_BLOCK, D_MODEL), lambda t: (cur(t)[0], jnp.minimum(cur(t)[1], n_lat - 1), 0)),
            pl.BlockSpec((None, CTX_LEN, D_MODEL), lambda t: (cur(t)[0], 0, 0)),
            pl.BlockSpec((None, None, 1, 3 * D_MODEL), mod_map),
            pl.BlockSpec((None, D_MODEL, IN_COLS_P), lambda t: (layer, 0, 0)),
            pl.BlockSpec((None, 8, 2 * LANES), lambda t: (layer, 0, 0)),
            pl.BlockSpec((None, 2 * LANES, MLA_HEADS * LANES), lambda t: (layer, 0, 0)),
            pl.BlockSpec((None, MLA_KV_RANK, 2 * MLA_HEADS * LANES), lambda t: (layer, 0, 0)),
            pl.BlockSpec((3, ROW_BLOCK, LANES), lambda t: (0, prev(t)[1], 0)),
            pl.BlockSpec((3, ROW_BLOCK, LANES), lambda t: (0, prev(t)[1], 0)),
        ],
        out_specs=[
            pl.BlockSpec((None, ROW_BLOCK, Q_COLS), lambda t: (*prev(t), 0)),
            pl.BlockSpec((None, KT_ROWS, ROW_BLOCK), lambda t: (prev(t)[0], 0, prev(t)[1])),
            pl.BlockSpec((None, ROW_BLOCK, V_COLS), lambda t: (*prev(t), 0)),
            pl.BlockSpec((None, ROW_BLOCK, MIX_WIDTH), lambda t: (*prev(t), 0)),
        ],
        out_shape=[
            jax.ShapeDtypeStruct((B, T, Q_COLS), jnp.bfloat16),
            jax.ShapeDtypeStruct((B, KT_ROWS, T), jnp.bfloat16),
            jax.ShapeDtypeStruct((B, T, V_COLS), jnp.bfloat16),
            jax.ShapeDtypeStruct((B, T, MIX_WIDTH), jnp.bfloat16),
        ],
        scratch_shapes=[pltpu.VMEM((ROW_BLOCK, IN_COLS_P), jnp.float32),
                        pltpu.VMEM((ROW_BLOCK, IN_COLS_P), jnp.float32)],
        compiler_params=pltpu.CompilerParams(
            dimension_semantics=("arbitrary",), vmem_limit_bytes=VMEM_LIMIT),
        name=f"proj_l{layer}",
    )(xl, xc, mod, w_in_p, gains, w_uq_p, w_ukv_p, cos_t, sin_t)


def _scores(qm, kt):
    return jnp.dot(qm, kt, preferred_element_type=jnp.float32)


def _probs(s):
    m = jnp.max(s, axis=-1, keepdims=True)
    return jnp.exp2(s - m).astype(jnp.bfloat16)


def _weighted_values(p, vext):
    o = jnp.dot(p, vext, preferred_element_type=jnp.float32)
    return o / o[:, 64:65]


def _softmax_maps():
    maps = []
    for hd in range(GQA_HEADS):
        c, half = hd % 3, hd // 3
        maps.append((Q_OFF_A + c * LANES, (32 * half, 32 * half + 32), KT_OFF_A, V_CH_A + half))
    for hd in range(DIFF_HEADS):
        c = hd // 2
        for j in (2 * (hd % 2), 2 * (hd % 2) + 1):
            maps.append((Q_OFF_D + c * LANES, (16 * j, 16 * j + 16), KT_OFF_D + c * LANES, V_CH_D + hd))
    for hd in range(MLA_HEADS):
        maps.append((Q_OFF_M + hd * LANES, None, KT_OFF_M + hd * LANES, V_CH_M + hd))
    return maps


N_MAPS = GQA_HEADS + 2 * DIFF_HEADS + MLA_HEADS


def _map_scores(i, q_ref, kt_ref):
    q_off, sel, kt_off, _ = _softmax_maps()[i]
    qc = q_ref[:, q_off:q_off + LANES]
    if sel is not None:
        lane = _lane_iota(qc.shape[0]) % 64
        qc = jnp.where((lane >= sel[0]) & (lane < sel[1]), qc, jnp.zeros((), qc.dtype))
    return _scores(qc, kt_ref[kt_off:kt_off + LANES, :])


def _map_values(i, p, v_ref):
    ch = _softmax_maps()[i][3]
    return _weighted_values(p, v_ref[:, ch * LANES:(ch + 1) * LANES])


def _gated_mix(outs, g_ref, vecs_ref, dl_ref, lam_init):
    rows = g_ref.shape[0]
    lo = _lane_iota(rows) < 64
    dl = dl_ref[...]
    lam = (jnp.exp(jnp.sum(dl[0:1] * dl[1:2], axis=-1, keepdims=True))
           - jnp.exp(jnp.sum(dl[2:3] * dl[3:4], axis=-1, keepdims=True)) + lam_init)
    subln = vecs_ref[2:3, 0:LANES] * (1.0 - lam_init)

    heads = outs[:GQA_HEADS]
    for hd in range(DIFF_HEADS):
        od = outs[GQA_HEADS + 2 * hd] - lam * outs[GQA_HEADS + 2 * hd + 1]
        ms = jnp.sum(jnp.where(lo, od * od, 0.0), axis=-1, keepdims=True) / DIFF_V_DIM
        heads.append(od * lax.rsqrt(ms + EPS) * subln)
    heads += outs[GQA_HEADS + 2 * DIFF_HEADS:]

    chunks = [jnp.where(lo, heads[2 * k], pltpu.roll(heads[2 * k + 1], 64, 1))
              for k in range(MIX_WIDTH // LANES)]
    hmix = jnp.concatenate(chunks, axis=-1) * g_ref[...].astype(jnp.float32)
    return hmix.astype(jnp.bfloat16)


def _project_residual_norm(hmix, x_ref, mod_ref, wout_ref, vecs_ref, o_ref):
    y = jnp.dot(hmix, wout_ref[...], preferred_element_type=jnp.float32)
    gate = mod_ref[...][:, 2 * D_MODEL:3 * D_MODEL]
    r = DEEPNORM_ALPHA * x_ref[...] + gate * y
    mu = jnp.mean(r, axis=-1, keepdims=True)
    rc = r - mu
    var = jnp.mean(rc * rc, axis=-1, keepdims=True)
    o_ref[...] = rc * lax.rsqrt(var + EPS) * vecs_ref[0:1] + vecs_ref[1:2]


def _attn_ctx_kernel(q_ref, kt_ref, v_ref, g_ref, x_ref, mod_ref, wout_ref, vecs_ref, dl_ref, o_ref, *, lam_init):
    all_p = [_probs(_map_scores(i, q_ref, kt_ref)) for i in range(N_MAPS)]
    outs = [_map_values(i, p, v_ref) for i, p in enumerate(all_p)]
    hmix = _gated_mix(outs, g_ref, vecs_ref, dl_ref, lam_init)
    _project_residual_norm(hmix, x_ref, mod_ref, wout_ref, vecs_ref, o_ref)


SCORES_AHEAD = 3
FINISH_AT = N_MAPS - 2


def _attn_latent_kernel(q_ref, kt_ref, v_ref, g_ref, x_ref, mod_ref, wout_ref, vecs_ref, dl_ref, o_ref,
                        outs_ref, *, lam_init, n_blocks):
    t = pl.program_id(0)

    @pl.when(t == 0)
    def _():
        outs_ref[...] = jnp.zeros(outs_ref.shape, outs_ref.dtype)

    def previous_mix():
        return _gated_mix([outs_ref[i] for i in range(N_MAPS)], g_ref, vecs_ref, dl_ref, lam_init)

    @pl.when(t < n_blocks)
    def _():
        pending = [_map_scores(i, q_ref, kt_ref) for i in range(SCORES_AHEAD)]
        hmix = previous_mix()
        p_cur = _probs(pending.pop(0))
        for i in range(N_MAPS):
            if i + SCORES_AHEAD < N_MAPS:
                pending.append(_map_scores(i + SCORES_AHEAD, q_ref, kt_ref))
            p_next = _probs(pending.pop(0)) if i + 1 < N_MAPS else None
            if i == FINISH_AT:
                _project_residual_norm(hmix, x_ref, mod_ref, wout_ref, vecs_ref, o_ref)
            outs_ref[i] = _map_values(i, p_cur, v_ref)
            p_cur = p_next

    @pl.when(t == n_blocks)
    def _():
        _project_residual_norm(previous_mix(), x_ref, mod_ref, wout_ref, vecs_ref, o_ref)


def _attn_latent_call(layer, q, kt, v, g, xl, mod, w_out_b, vecs, dl, lam_init):
    B, S, _ = xl.shape
    T = kt.shape[2]
    per_batch = S // ATTN_ROWS
    n_blocks = B * per_batch

    def cur(t):
        c = jnp.minimum(t, n_blocks - 1)
        return c // per_batch, c % per_batch

    def prev(t):
        p = jnp.maximum(t - 1, 0)
        return p // per_batch, p % per_batch

    kernel = functools.partial(_attn_latent_kernel, lam_init=lam_init, n_blocks=n_blocks)
    return pl.pallas_call(
        kernel,
        grid=(n_blocks + 1,),
        in_specs=[
            pl.BlockSpec((None, ATTN_ROWS, Q_COLS), lambda t: (*cur(t), 0)),
            pl.BlockSpec((None, KT_ROWS, T), lambda t: (cur(t)[0], 0, 0)),
            pl.BlockSpec((None, T, V_COLS), lambda t: (cur(t)[0], 0, 0)),
            pl.BlockSpec((None, ATTN_ROWS, MIX_WIDTH), lambda t: (*prev(t), 0)),
            pl.BlockSpec((None, ATTN_ROWS, D_MODEL), lambda t: (*prev(t), 0)),
            pl.BlockSpec((None, None, 1, 3 * D_MODEL), lambda t: (layer, prev(t)[0], 0, 0)),
            pl.BlockSpec((None, MIX_WIDTH, D_MODEL), lambda t: (layer, 0, 0), pipeline_mode=pl.Buffered(1)),
            pl.BlockSpec((None, 8, D_MODEL), lambda t: (layer, 0, 0)),
            pl.BlockSpec((None, 4, DIFF_QK_DIM), lambda t: (layer, 0, 0)),
        ],
        out_specs=pl.BlockSpec((None, ATTN_ROWS, D_MODEL), lambda t: (*prev(t), 0)),
        out_shape=jax.ShapeDtypeStruct((B, S, D_MODEL), jnp.float32),
        scratch_shapes=[pltpu.VMEM((N_MAPS, ATTN_ROWS, LANES), jnp.float32)],
        compiler_params=pltpu.CompilerParams(
            dimension_semantics=("arbitrary",), vmem_limit_bytes=VMEM_LIMIT),
        name=f"attn_latent_l{layer}",
    )(q, kt, v, g, xl, mod, w_out_b, vecs, dl)


def _attn_ctx_call(layer, q, kt, v, g, xc, mod, w_out_b, vecs, dl, lam_init):
    B = xc.shape[0]
    blk = (kt.shape[2] - CTX_LEN) // CTX_LEN
    kernel = functools.partial(_attn_ctx_kernel, lam_init=lam_init)
    return pl.pallas_call(
        kernel,
        grid=(B,),
        in_specs=[
            pl.BlockSpec((None, CTX_LEN, Q_COLS), lambda b: (b, blk, 0)),
            pl.BlockSpec((None, KT_ROWS, CTX_LEN), lambda b: (b, 0, blk)),
            pl.BlockSpec((None, CTX_LEN, V_COLS), lambda b: (b, blk, 0)),
            pl.BlockSpec((None, CTX_LEN, MIX_WIDTH), lambda b: (b, blk, 0)),
            pl.BlockSpec((None, CTX_LEN, D_MODEL), lambda b: (b, 0, 0)),
            pl.BlockSpec((None, None, 1, 3 * D_MODEL), lambda b: (layer, CTX_ROW, 0, 0)),
            pl.BlockSpec((None, MIX_WIDTH, D_MODEL), lambda b: (layer, 0, 0)),
            pl.BlockSpec((None, 8, D_MODEL), lambda b: (layer, 0, 0)),
            pl.BlockSpec((None, 4, DIFF_QK_DIM), lambda b: (layer, 0, 0)),
        ],
        out_specs=pl.BlockSpec((None, CTX_LEN, D_MODEL), lambda b: (b, 0, 0)),
        out_shape=jax.ShapeDtypeStruct((B, CTX_LEN, D_MODEL), jnp.float32),
        compiler_params=pltpu.CompilerParams(
            dimension_semantics=("arbitrary",), vmem_limit_bytes=VMEM_LIMIT),
        name=f"attn_ctx_l{layer}",
    )(q, kt, v, g, xc, mod, w_out_b, vecs, dl)


def _rope_tables(seq):
    rows_n = seq // GRID_W
    row = jnp.repeat(jnp.arange(rows_n, dtype=jnp.int32), GRID_W)
    col = jnp.tile(jnp.arange(GRID_W, dtype=jnp.int32), rows_n)

    def tables(pos, dim):
        freqs = ROPE_THETA ** (-jnp.arange(0, dim, 2, dtype=jnp.float32) / dim)
        ang = pos.astype(jnp.float32)[:, None] * freqs[None, :]
        return jnp.cos(ang), jnp.sin(ang)

    def head_tables(head_rot_dim):
        half = head_rot_dim // 2
        cr, sr = tables(row, half)
        cc, sc = tables(col, half)
        return (jnp.concatenate([cr, cr, cc, cc], axis=-1),
                jnp.concatenate([-sr, sr, -sc, sc], axis=-1))

    c64, s64 = head_tables(GQA_HEAD_DIM)
    c32, s32 = head_tables(DIFF_QK_DIM)
    ones, zeros = jnp.ones((seq, 64), jnp.float32), jnp.zeros((seq, 64), jnp.float32)
    perm_a, perm_d, perm_m = _pair_split_perms()
    cos_l = jnp.stack([jnp.tile(c64, (1, 2))[:, perm_a], jnp.tile(c32, (1, 4))[:, perm_d],
                       jnp.concatenate([ones, c32, ones[:, :32]], axis=-1)[:, perm_m]])
    sin_l = jnp.stack([jnp.tile(s64, (1, 2))[:, perm_a], jnp.tile(s32, (1, 4))[:, perm_d],
                       jnp.concatenate([zeros, s32, zeros[:, :32]], axis=-1)[:, perm_m]])
    cos_t = jnp.concatenate([cos_l, jnp.ones((3, CTX_LEN, LANES), jnp.float32)], axis=1)
    sin_t = jnp.concatenate([sin_l, jnp.zeros((3, CTX_LEN, LANES), jnp.float32)], axis=1)
    return cos_t, sin_t


def _pair_split_perms():
    def first_member(w, half):
        return w if w < half else 2 * half + (w - half)

    perm_a = np.zeros((LANES,), np.int64)
    perm_d = np.zeros((LANES,), np.int64)
    perm_m = np.full((LANES,), LANES - 1, np.int64)
    for lane in range(LANES):
        side, r = lane // 64, lane % 64
        perm_a[lane] = (r // 32) * 64 + first_member(r % 32, 16) + 16 * side
        perm_d[lane] = (r // 16) * 32 + first_member(r % 16, 8) + 8 * side
        if r < 16:
            perm_m[lane] = MLA_NOPE + first_member(r, 8) + 8 * side
        elif side == 0:
            perm_m[lane] = r - 16
        elif r < 32:
            perm_m[lane] = 48 + (r - 16)
    return perm_a, perm_d, perm_m


def _in_proj_columns():
    sizes = (384, 128, 128, 256, 256, 256, 192, 128, 32, 384, 256, 384)
    starts = np.concatenate([[0], np.cumsum(sizes)[:-1]])
    (s_gq, s_gk, s_gv, s_dq, s_dk, s_dv, s_mq, s_mkv, s_mkr, s_ga, _, _) = [int(s) for s in starts]
    cols = np.full((IN_COLS_P,), -1, np.int64)
    for c in range(3):
        cols[OFF_GQ + c * 128:OFF_GQ + c * 128 + 64] = s_gq + c * 64 + np.arange(64)
        cols[OFF_GQ + c * 128 + 64:OFF_GQ + (c + 1) * 128] = s_gq + (3 + c) * 64 + np.arange(64)
    cols[OFF_GK:OFF_GK + 128] = s_gk + np.arange(128)
    cols[OFF_GV:OFF_GV + 128] = s_gv + np.arange(128)
    cols[OFF_DQ:OFF_DQ + 256] = s_dq + np.arange(256)
    cols[OFF_DK:OFF_DK + 256] = s_dk + np.arange(256)
    cols[OFF_DV:OFF_DV + 256] = s_dv + np.arange(256)
    cols[OFF_GATE:OFF_GATE + 1024] = s_ga + np.arange(1024)
    cols[OFF_MKV:OFF_MKV + 128] = s_mkv + np.arange(128)
    cols[OFF_MQ:OFF_MQ + 192] = s_mq + np.arange(192)
    cols[OFF_MKR:OFF_MKR + 32] = s_mkr + np.concatenate([np.arange(8), 16 + np.arange(8),
                                                         8 + np.arange(8), 24 + np.arange(8)])
    perm_a, perm_d, _ = _pair_split_perms()
    for off, perm in ([(OFF_GQ + c * LANES, perm_a) for c in range(3)] + [(OFF_GK, perm_a)]
                      + [(o + c * LANES, perm_d) for o in (OFF_DQ, OFF_DK) for c in range(2)]):
        cols[off:off + LANES] = cols[off:off + LANES][perm]
    return cols


def _static_runs(cols):
    runs, start = [], 0
    for j in range(1, len(cols) + 1):
        if (j == len(cols) or (cols[j] < 0) != (cols[start] < 0)
                or (cols[start] >= 0 and cols[j] != cols[j - 1] + 1)):
            runs.append((int(cols[start]) if cols[start] >= 0 else -1, j - start))
            start = j
    return runs


def _in_proj_layout_kernel(wt_ref, o_ref):
    cols = _in_proj_columns()
    for j in range(IN_COLS_P // LANES):
        pieces = [jnp.zeros((n, wt_ref.shape[1]), jnp.float32) if src < 0 else wt_ref[src:src + n, :]
                  for src, n in _static_runs(cols[j * LANES:(j + 1) * LANES])]
        block = pieces[0] if len(pieces) == 1 else jnp.concatenate(pieces, axis=0)
        o_ref[:, j * LANES:(j + 1) * LANES] = block.T.astype(o_ref.dtype)


def _in_proj_layout_call(w_in):
    n_layers, d_in, n_src = w_in.shape
    return pl.pallas_call(
        _in_proj_layout_kernel,
        grid=(n_layers,),
        in_specs=[pl.BlockSpec((None, n_src, d_in), lambda l: (l, 0, 0))],
        out_specs=pl.BlockSpec((None, d_in, IN_COLS_P), lambda l: (l, 0, 0)),
        out_shape=jax.ShapeDtypeStruct((n_layers, d_in, IN_COLS_P), jnp.bfloat16),
        compiler_params=pltpu.CompilerParams(
            dimension_semantics=("arbitrary",), vmem_limit_bytes=VMEM_LIMIT),
        name="in_proj_layout",
    )(jnp.swapaxes(w_in, 1, 2))


def _gather_cols(w, cols, dtype):
    wt = jnp.swapaxes(w, -1, -2)
    parts = [jnp.zeros(wt.shape[:-2] + (n, wt.shape[-1]), wt.dtype) if src < 0 else wt[..., src:src + n, :]
             for src, n in _static_runs(cols)]
    return jnp.swapaxes(jnp.concatenate(parts, axis=-2).astype(dtype), -1, -2)


def kernel(x, c, ctx, c_ctx, w_ada, b_ada, w_in, gqa_q_g, gqa_k_g, diff_lq1, diff_lk1, diff_lq2, diff_lk2,
           diff_subln_g, mla_q_g, w_uq, mla_kv_g, w_ukv, w_out, ln_g, ln_b):
    B, S, D = x.shape
    f32, bf16 = jnp.float32, jnp.bfloat16

    w_in_p = _in_proj_layout_call(w_in)
    uq_cols = np.full((MLA_HEADS * LANES,), -1, np.int64)
    ukv_cols = np.full((2 * MLA_HEADS * LANES,), -1, np.int64)
    perm_a, _, perm_m = _pair_split_perms()
    for hh in range(MLA_HEADS):
        uq_cols[hh * LANES:hh * LANES + 96] = hh * 96 + np.arange(96)
        ukv_cols[hh * LANES:hh * LANES + 64] = hh * 128 + np.arange(64)
        ukv_cols[(MLA_HEADS + hh) * LANES:(MLA_HEADS + hh) * LANES + 64] = hh * 128 + 64 + np.arange(64)
        for cols in (uq_cols, ukv_cols):
            cols[hh * LANES:(hh + 1) * LANES] = cols[hh * LANES:(hh + 1) * LANES][perm_m]
    w_uq_p = jnp.pad(_gather_cols(w_uq, uq_cols, bf16), ((0, 0), (0, 2 * LANES - MLA_Q_RANK), (0, 0)))
    w_ukv_p = _gather_cols(w_ukv, ukv_cols, bf16)
    w_out_b = w_out.astype(bf16)

    def row(vec, width):
        return jnp.pad(vec, ((0, 0), (0, width - vec.shape[-1])))[:, None, :]

    gains = jnp.concatenate([
        row(jnp.tile(gqa_q_g, (1, 2))[:, perm_a], 2 * LANES), row(jnp.tile(gqa_k_g, (1, 2))[:, perm_a], 2 * LANES),
        row(mla_q_g, 2 * LANES), row(mla_kv_g, 2 * LANES),
        jnp.zeros((DEPTH, 4, 2 * LANES), f32)], axis=1)
    vecs = jnp.concatenate([
        row(ln_g, D), row(ln_b, D), row(jnp.tile(diff_subln_g, (1, 2)), D),
        jnp.zeros((DEPTH, 5, D), f32)], axis=1)
    dl = jnp.stack([diff_lq1, diff_lk1, diff_lq2, diff_lk2], axis=1)
    cos_t, sin_t = _rope_tables(S)

    cc = jnp.concatenate([c, c_ctx[None, :], jnp.zeros((ADA_ROWS - B - 1, D), f32)], axis=0)
    mod = _ada_call(cc, w_ada, b_ada[:, None, :])[:, :, None, :]

    xl, xc = x, ctx
    for layer in range(DEPTH):
        lam_init = 0.8 - 0.6 * math.exp(-0.3 * layer)
        q, kt, v, g = _proj_call(layer, xl, xc, mod, w_in_p, gains, w_uq_p, w_ukv_p, cos_t, sin_t)
        xl_new = _attn_latent_call(layer, q, kt, v, g, xl, mod, w_out_b, vecs, dl, lam_init)
        if layer < DEPTH - 1:
            xc = _attn_ctx_call(layer, q, kt, v, g, xc, mod, w_out_b, vecs, dl, lam_init)
        xl = xl_new
    return xl
```

```python
import functools
import math

import numpy as np
import jax
import jax.numpy as jnp
from jax import lax
from jax.experimental import pallas as pl
from jax.experimental.pallas import tpu as pltpu

D_MODEL = 1024
DEPTH = 4
GRID_W = 64
CTX_LEN = 256
ROPE_THETA = 10000.0
EPS = 1e-6

GQA_HEADS = 6
GQA_KV_HEADS = 2
GQA_HEAD_DIM = 64
DIFF_HEADS = 4
DIFF_QK_DIM = 32
DIFF_V_DIM = 64
MLA_HEADS = 6
MLA_NOPE = 64
MLA_ROPE = 32
MLA_V = 64
MLA_Q_RANK = 192
MLA_KV_RANK = 128
MIX_WIDTH = 1024

DEEPNORM_ALPHA = (2.0 * DEPTH) ** 0.25
LOG2_E = math.log2(math.e)
GQA_QSCALE = GQA_HEAD_DIM ** -0.5 * LOG2_E
DIFF_QSCALE = DIFF_QK_DIM ** -0.5 * LOG2_E
MLA_QSCALE = (MLA_NOPE + MLA_ROPE) ** -0.5 * LOG2_E

LANES = 128
ROW_BLOCK = 256
ATTN_ROWS = 256
ADA_ROWS = 16
CTX_ROW = 8
ADA_COL_BLOCK = 1536

OFF_GQ, OFF_GK, OFF_GV = 0, 384, 512
OFF_DQ, OFF_DK, OFF_DV = 640, 896, 1152
OFF_GATE = 1408
OFF_MKV = 2432
OFF_MQ = 2560
OFF_MKR = 2752
IN_COLS_P = 2816

Q_OFF_A, Q_OFF_D, Q_OFF_M = 0, 384, 640
Q_COLS = 1408
KT_OFF_A, KT_OFF_D, KT_OFF_M = 0, 128, 384
KT_ROWS = 1152
V_CH_A, V_CH_D, V_CH_M = 0, 2, 6
V_COLS = 12 * LANES

VMEM_LIMIT = 56 * 1024 * 1024


def _lane_iota(rows, cols=LANES):
    return lax.broadcasted_iota(jnp.int32, (rows, cols), 1)


def _rope(x, cos, sin):
    return x * cos + pltpu.roll(x, LANES // 2, 1) * sin


def _ada_kernel(cc_ref, w_ref, b_ref, o_ref):
    cc = cc_ref[...]
    a = (cc * jax.nn.sigmoid(cc)).astype(jnp.bfloat16)
    w = w_ref[...].astype(jnp.bfloat16)
    o_ref[...] = jnp.dot(a, w, preferred_element_type=jnp.float32) + b_ref[...]


def _ada_call(cc, w_ada, b_ada3):
    n_col = (3 * D_MODEL) // ADA_COL_BLOCK
    return pl.pallas_call(
        _ada_kernel,
        grid=(DEPTH, n_col),
        in_specs=[
            pl.BlockSpec((ADA_ROWS, D_MODEL), lambda l, j: (0, 0)),
            pl.BlockSpec((None, D_MODEL, ADA_COL_BLOCK), lambda l, j: (l, 0, j)),
            pl.BlockSpec((None, 1, ADA_COL_BLOCK), lambda l, j: (l, 0, j)),
        ],
        out_specs=pl.BlockSpec((None, ADA_ROWS, ADA_COL_BLOCK), lambda l, j: (l, 0, j)),
        out_shape=jax.ShapeDtypeStruct((DEPTH, ADA_ROWS, 3 * D_MODEL), jnp.float32),
        compiler_params=pltpu.CompilerParams(
            dimension_semantics=("arbitrary", "arbitrary"), vmem_limit_bytes=VMEM_LIMIT),
        name="ada_modulation",
    )(cc, w_ada, b_ada3)


def _modulated_in_proj(is_ctx, xl_ref, xc_ref, mod_ref, win_ref, z_ref):
    x = jnp.where(is_ctx, xc_ref[...], xl_ref[...])
    mu = jnp.mean(x, axis=-1, keepdims=True)
    xc = x - mu
    var = jnp.mean(xc * xc, axis=-1, keepdims=True)
    mod = mod_ref[...]
    shift = mod[:, 0:D_MODEL]
    scale = mod[:, D_MODEL:2 * D_MODEL]
    h = (xc * lax.rsqrt(var + EPS) * (1.0 + scale) + shift).astype(jnp.bfloat16)
    z_ref[...] = jnp.dot(h, win_ref[...], preferred_element_type=jnp.float32)


def _proj_kernel(xl_ref, xc_ref, mod_ref, win_ref, gains_ref, wuq_ref, wukv_ref, cos_ref, sin_ref,
                 q_ref, kt_ref, v_ref, g_ref, z_even, z_odd, *, n_blocks, blocks_per_batch):
    t = pl.program_id(0)
    is_ctx = jnp.minimum(t, n_blocks - 1) % blocks_per_batch == blocks_per_batch - 1

    @pl.when(t == 0)
    def _():
        z_odd[...] = jnp.zeros(z_odd.shape, z_odd.dtype)

    @pl.when(t % 2 == 0)
    def _():
        _modulated_in_proj(is_ctx, xl_ref, xc_ref, mod_ref, win_ref, z_even)
        _split_groups(z_odd, gains_ref, wuq_ref, wukv_ref, cos_ref, sin_ref, q_ref, kt_ref, v_ref, g_ref)

    @pl.when(t % 2 == 1)
    def _():
        _modulated_in_proj(is_ctx, xl_ref, xc_ref, mod_ref, win_ref, z_odd)
        _split_groups(z_even, gains_ref, wuq_ref, wukv_ref, cos_ref, sin_ref, q_ref, kt_ref, v_ref, g_ref)


def _split_groups(z, gains_ref, wuq_ref, wukv_ref, cos_ref, sin_ref, q_ref, kt_ref, v_ref, g_ref):
    rows = z.shape[0]
    lane = _lane_iota(rows)
    lo = lane < 64
    one_at_64 = (lane == 64).astype(jnp.float32)

    gains = gains_ref[...]
    g_gq = gains[0:1, 0:LANES]
    g_gk = gains[1:2, 0:LANES]
    g_mq = gains[2:3, :]
    g_mkv = gains[3:4, 0:LANES]
    cos_a, sin_a = cos_ref[0], sin_ref[0]
    cos_d, sin_d = cos_ref[1], sin_ref[1]
    cos_m, sin_m = cos_ref[2], sin_ref[2]

    first_head = (lane % 64) < 32

    def head_rms(zc, gain):
        sq = zc * zc
        s_a = jnp.sum(jnp.where(first_head, sq, 0.0), axis=-1, keepdims=True)
        s_b = jnp.sum(jnp.where(first_head, 0.0, sq), axis=-1, keepdims=True)
        r = jnp.where(first_head, lax.rsqrt(s_a / GQA_HEAD_DIM + EPS), lax.rsqrt(s_b / GQA_HEAD_DIM + EPS))
        return zc * r * gain

    for c in range(3):
        zc = z[:, OFF_GQ + c * LANES:OFF_GQ + (c + 1) * LANES]
        y = _rope(head_rms(zc, g_gq), cos_a, sin_a) * GQA_QSCALE
        q_ref[:, Q_OFF_A + c * LANES:Q_OFF_A + (c + 1) * LANES] = y.astype(q_ref.dtype)
    yk = _rope(head_rms(z[:, OFF_GK:OFF_GK + LANES], g_gk), cos_a, sin_a)
    kt_ref[KT_OFF_A:KT_OFF_A + LANES, :] = yk.T.astype(kt_ref.dtype)

    def store_v_pair(zv, chunk):
        v_lo = jnp.where(lo, zv, one_at_64)
        v_hi = jnp.where(lo, pltpu.roll(zv, 64, 1), one_at_64)
        v_ref[:, chunk * LANES:(chunk + 1) * LANES] = v_lo.astype(v_ref.dtype)
        v_ref[:, (chunk + 1) * LANES:(chunk + 2) * LANES] = v_hi.astype(v_ref.dtype)

    store_v_pair(z[:, OFF_GV:OFF_GV + LANES], V_CH_A)

    for c in range(2):
        zq = z[:, OFF_DQ + c * LANES:OFF_DQ + (c + 1) * LANES]
        yq = _rope(zq, cos_d, sin_d) * DIFF_QSCALE
        q_ref[:, Q_OFF_D + c * LANES:Q_OFF_D + (c + 1) * LANES] = yq.astype(q_ref.dtype)
        zk = z[:, OFF_DK + c * LANES:OFF_DK + (c + 1) * LANES]
        yk = _rope(zk, cos_d, sin_d)
        kt_ref[KT_OFF_D + c * LANES:KT_OFF_D + (c + 1) * LANES, :] = yk.T.astype(kt_ref.dtype)
        store_v_pair(z[:, OFF_DV + c * LANES:OFF_DV + (c + 1) * LANES], V_CH_D + 2 * c)

    zg = z[:, OFF_GATE:OFF_GATE + MIX_WIDTH]
    g_ref[...] = (zg * jax.nn.sigmoid(zg)).astype(g_ref.dtype)

    ckv = z[:, OFF_MKV:OFF_MKV + MLA_KV_RANK]
    ms = jnp.mean(ckv * ckv, axis=-1, keepdims=True)
    cn = (ckv * lax.rsqrt(ms + EPS) * g_mkv).astype(jnp.bfloat16)
    kv = jnp.dot(cn, wukv_ref[...], preferred_element_type=jnp.float32)
    last = z[:, IN_COLS_P - LANES:IN_COLS_P]
    kpe = jnp.where(lane < 16, pltpu.roll(last, 64, 1),
                    jnp.where((lane >= 64) & (lane < 80), pltpu.roll(last, LANES - 16, 1), 0.0))
    kpe = _rope(kpe, cos_m, sin_m)
    n_m = MLA_HEADS * LANES
    for hh in range(MLA_HEADS):
        k_h = kv[:, hh * LANES:(hh + 1) * LANES] + kpe
        kt_ref[KT_OFF_M + hh * LANES:KT_OFF_M + (hh + 1) * LANES, :] = k_h.T.astype(kt_ref.dtype)
        v_h = kv[:, n_m + hh * LANES:n_m + (hh + 1) * LANES] + one_at_64
        v_ref[:, (V_CH_M + hh) * LANES:(V_CH_M + hh + 1) * LANES] = v_h.astype(v_ref.dtype)

    zq = z[:, OFF_MQ:OFF_MQ + 2 * LANES]
    lane2 = _lane_iota(rows, 2 * LANES)
    msq = jnp.sum(jnp.where(lane2 < MLA_Q_RANK, zq * zq, 0.0), axis=-1, keepdims=True) / MLA_Q_RANK
    qn = (zq * lax.rsqrt(msq + EPS) * g_mq).astype(jnp.bfloat16)
    qm = jnp.dot(qn, wuq_ref[...], preferred_element_type=jnp.float32)
    for hh in range(MLA_HEADS):
        yq = _rope(qm[:, hh * LANES:(hh + 1) * LANES], cos_m, sin_m) * MLA_QSCALE
        q_ref[:, Q_OFF_M + hh * LANES:Q_OFF_M + (hh + 1) * LANES] = yq.astype(q_ref.dtype)


def _proj_call(layer, xl, xc, mod, w_in_p, gains, w_uq_p, w_ukv_p, cos_t, sin_t):
    B, S, _ = xl.shape
    n_lat = S // ROW_BLOCK
    nblk = n_lat + 1
    T = S + CTX_LEN
    n_blocks = B * nblk

    def cur(t):
        c = jnp.minimum(t, n_blocks - 1)
        return c // nblk, c % nblk

    def prev(t):
        p = jnp.maximum(t - 1, 0)
        return p // nblk, p % nblk

    def mod_map(t):
        b, i = cur(t)
        return (layer, jnp.where(i == n_lat, CTX_ROW, b), 0, 0)

    kernel = functools.partial(_proj_kernel, n_blocks=n_blocks, blocks_per_batch=nblk)
    return pl.pallas_call(
        kernel,
        grid=(n_blocks + 1,),
        in_specs=[
            pl.BlockSpec((None, ROW_BLOCK, D_MODEL), lambda t: (cur(t)[0], jnp.minimum(cur(t)[1], n_lat - 1), 0)),
            pl.BlockSpec((None, CTX_LEN, D_MODEL), lambda t: (cur(t)[0], 0, 0)),
            pl.BlockSpec((None, None, 1, 3 * D_MODEL), mod_map),
            pl.BlockSpec((None, D_MODEL, IN_COLS_P), lambda t: (layer, 0, 0)),
            pl.BlockSpec((None, 8, 2 * LANES), lambda t: (layer, 0, 0)),
            pl.BlockSpec((None, 2 * LANES, MLA_HEADS * LANES), lambda t: (layer, 0, 0)),
            pl.BlockSpec((None, MLA_KV_RANK, 2 * MLA_HEADS * LANES), lambda t: (layer, 0, 0)),
            pl.BlockSpec((3, ROW_BLOCK, LANES), lambda t: (0, prev(t)[1], 0)),
            pl.BlockSpec((3, ROW_BLOCK, LANES), lambda t: (0, prev(t)[1], 0)),
        ],
        out_specs=[
            pl.BlockSpec((None, ROW_BLOCK, Q_COLS), lambda t: (*prev(t), 0)),
            pl.BlockSpec((None, KT_ROWS, ROW_BLOCK), lambda t: (prev(t)[0], 0, prev(t)[1])),
            pl.BlockSpec((None, ROW_BLOCK, V_COLS), lambda t: (*prev(t), 0)),
            pl.BlockSpec((None, ROW_BLOCK, MIX_WIDTH), lambda t: (*prev(t), 0)),
        ],
        out_shape=[
            jax.ShapeDtypeStruct((B, T, Q_COLS), jnp.bfloat16),
            jax.ShapeDtypeStruct((B, KT_ROWS, T), jnp.bfloat16),
            jax.ShapeDtypeStruct((B, T, V_COLS), jnp.bfloat16),
            jax.ShapeDtypeStruct((B, T, MIX_WIDTH), jnp.bfloat16),
        ],
        scratch_shapes=[pltpu.VMEM((ROW_BLOCK, IN_COLS_P), jnp.float32),
                        pltpu.VMEM((ROW_BLOCK, IN_COLS_P), jnp.float32)],
        compiler_params=pltpu.CompilerParams(
            dimension_semantics=("arbitrary",), vmem_limit_bytes=VMEM_LIMIT),
        name=f"proj_l{layer}",
    )(xl, xc, mod, w_in_p, gains, w_uq_p, w_ukv_p, cos_t, sin_t)


def _scores(qm, kt):
    return jnp.dot(qm, kt, preferred_element_type=jnp.float32)


def _probs(s):
    m = jnp.max(s, axis=-1, keepdims=True)
    return jnp.exp2(s - m).astype(jnp.bfloat16)


def _weighted_values(p, vext):
    o = jnp.dot(p, vext, preferred_element_type=jnp.float32)
    return o / o[:, 64:65]


def _softmax_maps():
    maps = []
    for hd in range(GQA_HEADS):
        c, half = hd % 3, hd // 3
        maps.append((Q_OFF_A + c * LANES, (32 * half, 32 * half + 32), KT_OFF_A, V_CH_A + half))
    for hd in range(DIFF_HEADS):
        c = hd // 2
        for j in (2 * (hd % 2), 2 * (hd % 2) + 1):
            maps.append((Q_OFF_D + c * LANES, (16 * j, 16 * j + 16), KT_OFF_D + c * LANES, V_CH_D + hd))
    for hd in range(MLA_HEADS):
        maps.append((Q_OFF_M + hd * LANES, None, KT_OFF_M + hd * LANES, V_CH_M + hd))
    return maps


N_MAPS = GQA_HEADS + 2 * DIFF_HEADS + MLA_HEADS


def _map_scores(i, q_ref, kt_ref):
    q_off, sel, kt_off, _ = _softmax_maps()[i]
    qc = q_ref[:, q_off:q_off + LANES]
    if sel is not None:
        lane = _lane_iota(qc.shape[0]) % 64
        qc = jnp.where((lane >= sel[0]) & (lane < sel[1]), qc, jnp.zeros((), qc.dtype))
    return _scores(qc, kt_ref[kt_off:kt_off + LANES, :])


def _map_values(i, p, v_ref):
    ch = _softmax_maps()[i][3]
    return _weighted_values(p, v_ref[:, ch * LANES:(ch + 1) * LANES])


def _gated_mix(outs, g_ref, vecs_ref, dl_ref, lam_init):
    rows = g_ref.shape[0]
    lo = _lane_iota(rows) < 64
    dl = dl_ref[...]
    lam = (jnp.exp(jnp.sum(dl[0:1] * dl[1:2], axis=-1, keepdims=True))
           - jnp.exp(jnp.sum(dl[2:3] * dl[3:4], axis=-1, keepdims=True)) + lam_init)
    subln = vecs_ref[2:3, 0:LANES] * (1.0 - lam_init)

    heads = outs[:GQA_HEADS]
    for hd in range(DIFF_HEADS):
        od = outs[GQA_HEADS + 2 * hd] - lam * outs[GQA_HEADS + 2 * hd + 1]
        ms = jnp.sum(jnp.where(lo, od * od, 0.0), axis=-1, keepdims=True) / DIFF_V_DIM
        heads.append(od * lax.rsqrt(ms + EPS) * subln)
    heads += outs[GQA_HEADS + 2 * DIFF_HEADS:]

    chunks = [jnp.where(lo, heads[2 * k], pltpu.roll(heads[2 * k + 1], 64, 1))
              for k in range(MIX_WIDTH // LANES)]
    hmix = jnp.concatenate(chunks, axis=-1) * g_ref[...].astype(jnp.float32)
    return hmix.astype(jnp.bfloat16)


def _project_residual_norm(hmix, x_ref, mod_ref, wout_ref, vecs_ref, o_ref):
    y = jnp.dot(hmix, wout_ref[...], preferred_element_type=jnp.float32)
    gate = mod_ref[...][:, 2 * D_MODEL:3 * D_MODEL]
    r = DEEPNORM_ALPHA * x_ref[...] + gate * y
    mu = jnp.mean(r, axis=-1, keepdims=True)
    rc = r - mu
    var = jnp.mean(rc * rc, axis=-1, keepdims=True)
    o_ref[...] = rc * lax.rsqrt(var + EPS) * vecs_ref[0:1] + vecs_ref[1:2]


def _attn_ctx_kernel(q_ref, kt_ref, v_ref, g_ref, x_ref, mod_ref, wout_ref, vecs_ref, dl_ref, o_ref, *, lam_init):
    all_p = [_probs(_map_scores(i, q_ref, kt_ref)) for i in range(N_MAPS)]
    outs = [_map_values(i, p, v_ref) for i, p in enumerate(all_p)]
    hmix = _gated_mix(outs, g_ref, vecs_ref, dl_ref, lam_init)
    _project_residual_norm(hmix, x_ref, mod_ref, wout_ref, vecs_ref, o_ref)


SCORES_AHEAD = 4
FINISH_AT = N_MAPS - 2


def _attn_latent_kernel(q_ref, kt_ref, v_ref, g_ref, x_ref, mod_ref, wout_ref, vecs_ref, dl_ref, o_ref,
                        outs_ref, *, lam_init, n_blocks):
    t = pl.program_id(0)

    @pl.when(t == 0)
    def _():
        outs_ref[...] = jnp.zeros(outs_ref.shape, outs_ref.dtype)

    def previous_mix():
        return _gated_mix([outs_ref[i] for i in range(N_MAPS)], g_ref, vecs_ref, dl_ref, lam_init)

    @pl.when(t < n_blocks)
    def _():
        pending = [_map_scores(i, q_ref, kt_ref) for i in range(SCORES_AHEAD)]
        hmix = previous_mix()
        p_cur = _probs(pending.pop(0))
        for i in range(N_MAPS):
            if i + SCORES_AHEAD < N_MAPS:
                pending.append(_map_scores(i + SCORES_AHEAD, q_ref, kt_ref))
            p_next = _probs(pending.pop(0)) if i + 1 < N_MAPS else None
            if i == FINISH_AT:
                _project_residual_norm(hmix, x_ref, mod_ref, wout_ref, vecs_ref, o_ref)
            outs_ref[i] = _map_values(i, p_cur, v_ref)
            p_cur = p_next

    @pl.when(t == n_blocks)
    def _():
        _project_residual_norm(previous_mix(), x_ref, mod_ref, wout_ref, vecs_ref, o_ref)


def _attn_latent_call(layer, q, kt, v, g, xl, mod, w_out_b, vecs, dl, lam_init):
    B, S, _ = xl.shape
    T = kt.shape[2]
    per_batch = S // ATTN_ROWS
    n_blocks = B * per_batch

    def cur(t):
        c = jnp.minimum(t, n_blocks - 1)
        return c // per_batch, c % per_batch

    def prev(t):
        p = jnp.maximum(t - 1, 0)
        return p // per_batch, p % per_batch

    kernel = functools.partial(_attn_latent_kernel, lam_init=lam_init, n_blocks=n_blocks)
    return pl.pallas_call(
        kernel,
        grid=(n_blocks + 1,),
        in_specs=[
            pl.BlockSpec((None, ATTN_ROWS, Q_COLS), lambda t: (*cur(t), 0)),
            pl.BlockSpec((None, KT_ROWS, T), lambda t: (cur(t)[0], 0, 0)),
            pl.BlockSpec((None, T, V_COLS), lambda t: (cur(t)[0], 0, 0)),
            pl.BlockSpec((None, ATTN_ROWS, MIX_WIDTH), lambda t: (*prev(t), 0)),
            pl.BlockSpec((None, ATTN_ROWS, D_MODEL), lambda t: (*prev(t), 0)),
            pl.BlockSpec((None, None, 1, 3 * D_MODEL), lambda t: (layer, prev(t)[0], 0, 0)),
            pl.BlockSpec((None, MIX_WIDTH, D_MODEL), lambda t: (layer, 0, 0), pipeline_mode=pl.Buffered(1)),
            pl.BlockSpec((None, 8, D_MODEL), lambda t: (layer, 0, 0)),
            pl.BlockSpec((None, 4, DIFF_QK_DIM), lambda t: (layer, 0, 0)),
        ],
        out_specs=pl.BlockSpec((None, ATTN_ROWS, D_MODEL), lambda t: (*prev(t), 0)),
        out_shape=jax.ShapeDtypeStruct((B, S, D_MODEL), jnp.float32),
        scratch_shapes=[pltpu.VMEM((N_MAPS, ATTN_ROWS, LANES), jnp.float32)],
        compiler_params=pltpu.CompilerParams(
            dimension_semantics=("arbitrary",), vmem_limit_bytes=VMEM_LIMIT),
        name=f"attn_latent_l{layer}",
    )(q, kt, v, g, xl, mod, w_out_b, vecs, dl)


def _attn_ctx_call(layer, q, kt, v, g, xc, mod, w_out_b, vecs, dl, lam_init):
    B = xc.shape[0]
    blk = (kt.shape[2] - CTX_LEN) // CTX_LEN
    kernel = functools.partial(_attn_ctx_kernel, lam_init=lam_init)
    return pl.pallas_call(
        kernel,
        grid=(B,),
        in_specs=[
            pl.BlockSpec((None, CTX_LEN, Q_COLS), lambda b: (b, blk, 0)),
            pl.BlockSpec((None, KT_ROWS, CTX_LEN), lambda b: (b, 0, blk)),
            pl.BlockSpec((None, CTX_LEN, V_COLS), lambda b: (b, blk, 0)),
            pl.BlockSpec((None, CTX_LEN, MIX_WIDTH), lambda b: (b, blk, 0)),
            pl.BlockSpec((None, CTX_LEN, D_MODEL), lambda b: (b, 0, 0)),
            pl.BlockSpec((None, None, 1, 3 * D_MODEL), lambda b: (layer, CTX_ROW, 0, 0)),
            pl.BlockSpec((None, MIX_WIDTH, D_MODEL), lambda b: (layer, 0, 0)),
            pl.BlockSpec((None, 8, D_MODEL), lambda b: (layer, 0, 0)),
            pl.BlockSpec((None, 4, DIFF_QK_DIM), lambda b: (layer, 0, 0)),
        ],
        out_specs=pl.BlockSpec((None, CTX_LEN, D_MODEL), lambda b: (b, 0, 0)),
        out_shape=jax.ShapeDtypeStruct((B, CTX_LEN, D_MODEL), jnp.float32),
        compiler_params=pltpu.CompilerParams(
            dimension_semantics=("arbitrary",), vmem_limit_bytes=VMEM_LIMIT),
        name=f"attn_ctx_l{layer}",
    )(q, kt, v, g, xc, mod, w_out_b, vecs, dl)


def _rope_tables(seq):
    rows_n = seq // GRID_W
    row = jnp.repeat(jnp.arange(rows_n, dtype=jnp.int32), GRID_W)
    col = jnp.tile(jnp.arange(GRID_W, dtype=jnp.int32), rows_n)

    def tables(pos, dim):
        freqs = ROPE_THETA ** (-jnp.arange(0, dim, 2, dtype=jnp.float32) / dim)
        ang = pos.astype(jnp.float32)[:, None] * freqs[None, :]
        return jnp.cos(ang), jnp.sin(ang)

    def head_tables(head_rot_dim):
        half = head_rot_dim // 2
        cr, sr = tables(row, half)
        cc, sc = tables(col, half)
        return (jnp.concatenate([cr, cr, cc, cc], axis=-1),
                jnp.concatenate([-sr, sr, -sc, sc], axis=-1))

    c64, s64 = head_tables(GQA_HEAD_DIM)
    c32, s32 = head_tables(DIFF_QK_DIM)
    ones, zeros = jnp.ones((seq, 64), jnp.float32), jnp.zeros((seq, 64), jnp.float32)
    perm_a, perm_d, perm_m = _pair_split_perms()
    cos_l = jnp.stack([jnp.tile(c64, (1, 2))[:, perm_a], jnp.tile(c32, (1, 4))[:, perm_d],
                       jnp.concatenate([ones, c32, ones[:, :32]], axis=-1)[:, perm_m]])
    sin_l = jnp.stack([jnp.tile(s64, (1, 2))[:, perm_a], jnp.tile(s32, (1, 4))[:, perm_d],
                       jnp.concatenate([zeros, s32, zeros[:, :32]], axis=-1)[:, perm_m]])
    cos_t = jnp.concatenate([cos_l, jnp.ones((3, CTX_LEN, LANES), jnp.float32)], axis=1)
    sin_t = jnp.concatenate([sin_l, jnp.zeros((3, CTX_LEN, LANES), jnp.float32)], axis=1)
    return cos_t, sin_t


def _pair_split_perms():
    def first_member(w, half):
        return w if w < half else 2 * half + (w - half)

    perm_a = np.zeros((LANES,), np.int64)
    perm_d = np.zeros((LANES,), np.int64)
    perm_m = np.full((LANES,), LANES - 1, np.int64)
    for lane in range(LANES):
        side, r = lane // 64, lane % 64
        perm_a[lane] = (r // 32) * 64 + first_member(r % 32, 16) + 16 * side
        perm_d[lane] = (r // 16) * 32 + first_member(r % 16, 8) + 8 * side
        if r < 16:
            perm_m[lane] = MLA_NOPE + first_member(r, 8) + 8 * side
        elif side == 0:
            perm_m[lane] = r - 16
        elif r < 32:
            perm_m[lane] = 48 + (r - 16)
    return perm_a, perm_d, perm_m


def _in_proj_columns():
    sizes = (384, 128, 128, 256, 256, 256, 192, 128, 32, 384, 256, 384)
    starts = np.concatenate([[0], np.cumsum(sizes)[:-1]])
    (s_gq, s_gk, s_gv, s_dq, s_dk, s_dv, s_mq, s_mkv, s_mkr, s_ga, _, _) = [int(s) for s in starts]
    cols = np.full((IN_COLS_P,), -1, np.int64)
    for c in range(3):
        cols[OFF_GQ + c * 128:OFF_GQ + c * 128 + 64] = s_gq + c * 64 + np.arange(64)
        cols[OFF_GQ + c * 128 + 64:OFF_GQ + (c + 1) * 128] = s_gq + (3 + c) * 64 + np.arange(64)
    cols[OFF_GK:OFF_GK + 128] = s_gk + np.arange(128)
    cols[OFF_GV:OFF_GV + 128] = s_gv + np.arange(128)
    cols[OFF_DQ:OFF_DQ + 256] = s_dq + np.arange(256)
    cols[OFF_DK:OFF_DK + 256] = s_dk + np.arange(256)
    cols[OFF_DV:OFF_DV + 256] = s_dv + np.arange(256)
    cols[OFF_GATE:OFF_GATE + 1024] = s_ga + np.arange(1024)
    cols[OFF_MKV:OFF_MKV + 128] = s_mkv + np.arange(128)
    cols[OFF_MQ:OFF_MQ + 192] = s_mq + np.arange(192)
    cols[OFF_MKR:OFF_MKR + 32] = s_mkr + np.concatenate([np.arange(8), 16 + np.arange(8),
                                                         8 + np.arange(8), 24 + np.arange(8)])
    perm_a, perm_d, _ = _pair_split_perms()
    for off, perm in ([(OFF_GQ + c * LANES, perm_a) for c in range(3)] + [(OFF_GK, perm_a)]
                      + [(o + c * LANES, perm_d) for o in (OFF_DQ, OFF_DK) for c in range(2)]):
        cols[off:off + LANES] = cols[off:off + LANES][perm]
    return cols


def _static_runs(cols):
    runs, start = [], 0
    for j in range(1, len(cols) + 1):
        if (j == len(cols) or (cols[j] < 0) != (cols[start] < 0)
                or (cols[start] >= 0 and cols[j] != cols[j - 1] + 1)):
            runs.append((int(cols[start]) if cols[start] >= 0 else -1, j - start))
            start = j
    return runs


def _in_proj_layout_kernel(wt_ref, o_ref):
    cols = _in_proj_columns()
    for j in range(IN_COLS_P // LANES):
        pieces = [jnp.zeros((n, wt_ref.shape[1]), jnp.float32) if src < 0 else wt_ref[src:src + n, :]
                  for src, n in _static_runs(cols[j * LANES:(j + 1) * LANES])]
        block = pieces[0] if len(pieces) == 1 else jnp.concatenate(pieces, axis=0)
        o_ref[:, j * LANES:(j + 1) * LANES] = block.T.astype(o_ref.dtype)


def _in_proj_layout_call(w_in):
    n_layers, d_in, n_src = w_in.shape
    return pl.pallas_call(
        _in_proj_layout_kernel,
        grid=(n_layers,),
        in_specs=[pl.BlockSpec((None, n_src, d_in), lambda l: (l, 0, 0))],
        out_specs=pl.BlockSpec((None, d_in, IN_COLS_P), lambda l: (l, 0, 0)),
        out_shape=jax.ShapeDtypeStruct((n_layers, d_in, IN_COLS_P), jnp.bfloat16),
        compiler_params=pltpu.CompilerParams(
            dimension_semantics=("arbitrary",), vmem_limit_bytes=VMEM_LIMIT),
        name="in_proj_layout",
    )(jnp.swapaxes(w_in, 1, 2))


def _gather_cols(w, cols, dtype):
    wt = jnp.swapaxes(w, -1, -2)
    parts = [jnp.zeros(wt.shape[:-2] + (n, wt.shape[-1]), wt.dtype) if src < 0 else wt[..., src:src + n, :]
             for src, n in _static_runs(cols)]
    return jnp.swapaxes(jnp.concatenate(parts, axis=-2).astype(dtype), -1, -2)


def kernel(x, c, ctx, c_ctx, w_ada, b_ada, w_in, gqa_q_g, gqa_k_g, diff_lq1, diff_lk1, diff_lq2, diff_lk2,
           diff_subln_g, mla_q_g, w_uq, mla_kv_g, w_ukv, w_out, ln_g, ln_b):
    B, S, D = x.shape
    f32, bf16 = jnp.float32, jnp.bfloat16

    w_in_p = _in_proj_layout_call(w_in)
    uq_cols = np.full((MLA_HEADS * LANES,), -1, np.int64)
    ukv_cols = np.full((2 * MLA_HEADS * LANES,), -1, np.int64)
    perm_a, _, perm_m = _pair_split_perms()
    for hh in range(MLA_HEADS):
        uq_cols[hh * LANES:hh * LANES + 96] = hh * 96 + np.arange(96)
        ukv_cols[hh * LANES:hh * LANES + 64] = hh * 128 + np.arange(64)
        ukv_cols[(MLA_HEADS + hh) * LANES:(MLA_HEADS + hh) * LANES + 64] = hh * 128 + 64 + np.arange(64)
        for cols in (uq_cols, ukv_cols):
            cols[hh * LANES:(hh + 1) * LANES] = cols[hh * LANES:(hh + 1) * LANES][perm_m]
    w_uq_p = jnp.pad(_gather_cols(w_uq, uq_cols, bf16), ((0, 0), (0, 2 * LANES - MLA_Q_RANK), (0, 0)))
    w_ukv_p = _gather_cols(w_ukv, ukv_cols, bf16)
    w_out_b = w_out.astype(bf16)

    def row(vec, width):
        return jnp.pad(vec, ((0, 0), (0, width - vec.shape[-1])))[:, None, :]

    gains = jnp.concatenate([
        row(jnp.tile(gqa_q_g, (1, 2))[:, perm_a], 2 * LANES), row(jnp.tile(gqa_k_g, (1, 2))[:, perm_a], 2 * LANES),
        row(mla_q_g, 2 * LANES), row(mla_kv_g, 2 * LANES),
        jnp.zeros((DEPTH, 4, 2 * LANES), f32)], axis=1)
    vecs = jnp.concatenate([
        row(ln_g, D), row(ln_b, D), row(jnp.tile(diff_subln_g, (1, 2)), D),
        jnp.zeros((DEPTH, 5, D), f32)], axis=1)
    dl = jnp.stack([diff_lq1, diff_lk1, diff_lq2, diff_lk2], axis=1)
    cos_t, sin_t = _rope_tables(S)

    cc = jnp.concatenate([c, c_ctx[None, :], jnp.zeros((ADA_ROWS - B - 1, D), f32)], axis=0)
    mod = _ada_call(cc, w_ada, b_ada[:, None, :])[:, :, None, :]

    xl, xc = x, ctx
    for layer in range(DEPTH):
        lam_init = 0.8 - 0.6 * math.exp(-0.3 * layer)
        q, kt, v, g = _proj_call(layer, xl, xc, mod, w_in_p, gains, w_uq_p, w_ukv_p, cos_t, sin_t)
        xl_new = _attn_latent_call(layer, q, kt, v, g, xl, mod, w_out_b, vecs, dl, lam_init)
        if layer < DEPTH - 1:
            xc = _attn_ctx_call(layer, q, kt, v, g, xc, mod, w_out_b, vecs, dl, lam_init)
        xl = xl_new
    return xl
```
